```python
import math, functools
import jax, jax.numpy as jnp
from jax import lax
import numpy as np

D_MODEL = 1024
BATCH = 16
SEQ = 2048
DEPTH = 2
DEC_BATCH = 32
DEC_SEQ = 4
PAST_LEN = 16384
PAGE_SIZE = 128

DA_HEADS = 4
DA_HD = 64
DA_VD = 2 * DA_HD
Q_BLOCK = 128
GDN_HEADS = 4
GDN_DK = 128
GDN_DV = 128
CONV_W = 4
GDN_CHUNK = 64
REL_BUCKETS = 32
REL_MAX_DIST = 128
PEER_HEADS = 8
N_KEYS = 128
N_EXPERTS = N_KEYS * N_KEYS
PEER_TOPK = 16
PEER_QDIM = 256
PEER_BLOCK = 128
EPS = 1e-6

DA_QK_W = DA_HEADS * 2 * DA_HD
DA_V_W = DA_HEADS * DA_VD
GDN_QKV = GDN_HEADS * (2 * GDN_DK + GDN_DV)
GDN_Z_W = GDN_HEADS * GDN_DV
IN_DIM = 2 * DA_QK_W + DA_V_W + GDN_QKV + GDN_Z_W + 2 * GDN_HEADS
MIX_W = DA_V_W + GDN_HEADS * GDN_DV

kernel_name = "hymba_diffattn_gdn_peer_step"


def rms_norm(x, g):
    xf = x.astype(jnp.float32)
    y = xf * lax.rsqrt(jnp.mean(xf * xf, axis=-1, keepdims=True) + EPS)
    return (y * g.astype(jnp.float32)).astype(x.dtype)


def l2_norm(x):
    xf = x.astype(jnp.float32)
    return (xf * lax.rsqrt(jnp.sum(xf * xf, axis=-1, keepdims=True) + EPS)).astype(x.dtype)


def rel_bucket(n):
    n = jnp.maximum(n, 0)
    max_exact = REL_BUCKETS // 2
    nf = jnp.maximum(n, 1).astype(jnp.float32)
    large = max_exact + (jnp.log(nf / max_exact) / math.log(REL_MAX_DIST / max_exact)
                         * (REL_BUCKETS - max_exact)).astype(jnp.int32)
    large = jnp.minimum(large, REL_BUCKETS - 1)
    return jnp.where(n < max_exact, n, large)


def diff_attend(q, k, v, q_pos, k_pos, rel_bias, lam):
    dist = q_pos[:, None] - k_pos[None, :]
    bias = jnp.transpose(rel_bias.astype(jnp.float32)[rel_bucket(dist)], (2, 0, 1))
    s = jnp.einsum('bqhsd,bkhsd->bshqk', q, k).astype(jnp.float32) * (DA_HD ** -0.5)
    s = jnp.where(dist >= 0, s + bias, -jnp.inf)
    a = jax.nn.softmax(s, axis=-1)
    w = a[:, 0] - lam * a[:, 1]
    return jnp.einsum('bhqk,bkhe->bqhe', w.astype(v.dtype), v)


def attend_prompt(q, k, v, lam, rel_bias):
    B, L = q.shape[0], q.shape[1]
    qb_len = min(Q_BLOCK, L)
    nb = L // qb_len
    qb = jnp.moveaxis(q.reshape(B, nb, qb_len, DA_HEADS, 2, DA_HD), 1, 0)
    starts = jnp.arange(nb, dtype=jnp.int32) * qb_len
    k_pos = jnp.arange(L, dtype=jnp.int32)

    def block(args):
        qi, s0 = args
        q_pos = s0 + jnp.arange(qb_len, dtype=jnp.int32)
        return diff_attend(qi, k, v, q_pos, k_pos, rel_bias, lam)

    o = lax.map(block, (qb, starts))
    return jnp.moveaxis(o, 0, 1).reshape(B, L, DA_HEADS, DA_VD)


def attend_with_past(q, k, v, lam, k_past, v_past, rel_bias):
    past = k_past.shape[1]
    Lq = q.shape[1]
    k_all = jnp.concatenate([k_past.astype(k.dtype), k], axis=1)
    v_all = jnp.concatenate([v_past.astype(v.dtype), v], axis=1)
    q_pos = past + jnp.arange(Lq, dtype=jnp.int32)
    k_pos = jnp.arange(past + Lq, dtype=jnp.int32)
    return diff_attend(q, k_all, v_all, q_pos, k_pos, rel_bias, lam)


def causal_conv_silu(xb, w):
    L = xb.shape[1] - (CONV_W - 1)
    y = sum(xb[:, j:j + L] * w[j] for j in range(CONV_W))
    return jax.nn.silu(y)


def gated_delta_rule(q, k, v, beta, g, s0):
    out_dtype = v.dtype
    B, L, H, dk = q.shape
    dv = v.shape[-1]
    C = min(GDN_CHUNK, L)
    n = -(-L // C)
    pad = n * C - L

    def prep(t):
        t = jnp.pad(t.astype(jnp.float32), [(0, 0), (0, pad)] + [(0, 0)] * (t.ndim - 2))
        t = t.reshape((B, n, C) + t.shape[2:])
        return jnp.moveaxis(t, (1, 3), (0, 2))

    qc, kc, vc, bc, gc = prep(q), prep(k), prep(v), prep(beta), prep(g)
    gc = jnp.cumsum(gc, axis=-1)
    idx = jnp.arange(C)
    incl = idx[:, None] >= idx[None, :]
    strict = idx[:, None] > idx[None, :]
    decay = jnp.exp(jnp.where(incl, gc[..., :, None] - gc[..., None, :], -jnp.inf))
    kb = kc * bc[..., None]
    a_mat = jnp.where(strict, jnp.einsum('...id,...jd->...ij', kb, kc) * decay, 0.0)
    eye = jnp.eye(C, dtype=jnp.float32)
    rhs = jnp.concatenate([vc * bc[..., None], kb * jnp.exp(gc)[..., None]], axis=-1)
    sol = lax.linalg.triangular_solve(eye + a_mat, rhs, left_side=True, lower=True)
    u, w = sol[..., :dv], sol[..., dv:]
    qk = jnp.einsum('...id,...jd->...ij', qc, kc) * decay

    def step(S, inp):
        q_i, k_i, u_i, w_i, g_i, qk_i = inp
        v_new = u_i - jnp.einsum('bhcd,bhde->bhce', w_i, S)
        o = (jnp.einsum('bhcd,bhde->bhce', q_i * jnp.exp(g_i)[..., None], S)
             + jnp.einsum('bhij,bhje->bhie', qk_i, v_new))
        g_last = g_i[..., -1]
        S = (S * jnp.exp(g_last)[..., None, None]
             + jnp.einsum('bhcd,bhce->bhde', k_i * jnp.exp(g_last[..., None] - g_i)[..., None], v_new))
        return S, o

    S, o = lax.scan(step, s0.astype(jnp.float32), (qc, kc, u, w, gc, qk))
    o = jnp.moveaxis(o, (0, 2), (1, 3)).reshape(B, n * C, H, dv)[:, :L]
    return o.astype(out_dtype), S.astype(out_dtype)


def peer_ffn(xn, w_q, sub_keys, u_tab, v_tab):
    B, L, D = xn.shape
    T = B * L
    nblk = -(-T // PEER_BLOCK)
    xt = jnp.pad(xn.reshape(T, D), ((0, nblk * PEER_BLOCK - T), (0, 0))).reshape(nblk, PEER_BLOCK, D)

    def block(xb):
        q = (xb @ w_q).reshape(PEER_BLOCK, PEER_HEADS, 2, PEER_QDIM // 2)
        s = jnp.einsum('thsd,hskd->thsk', q, sub_keys).astype(jnp.float32)
        s1, i1 = lax.top_k(s[:, :, 0], PEER_TOPK)
        s2, i2 = lax.top_k(s[:, :, 1], PEER_TOPK)
        cand = (s1[..., :, None] + s2[..., None, :]).reshape(PEER_BLOCK, PEER_HEADS, PEER_TOPK * PEER_TOPK)
        cid = (i1[..., :, None] * N_KEYS + i2[..., None, :]).reshape(PEER_BLOCK, PEER_HEADS, PEER_TOPK * PEER_TOPK)
        top_s, top_pos = lax.top_k(cand, PEER_TOPK)
        eid = jnp.take_along_axis(cid, top_pos, axis=-1)
        gate = jax.nn.softmax(top_s, axis=-1)
        hid = jax.nn.gelu(jnp.einsum('thkd,td->thk', u_tab[eid], xb).astype(jnp.float32))
        return jnp.einsum('thk,thkd->td', (gate * hid).astype(xb.dtype), v_tab[eid])

    out = lax.map(block, xt)
    return out.reshape(nblk * PEER_BLOCK, D)[:T].reshape(B, L, D)


def trunk_layer(x, conv_buf, ssm_state, attend, lambda_init,
                norm1_g, w_in, da_q_g, da_k_g, lambda_q1, lambda_k1, lambda_q2, lambda_k2, da_subln_g,
                gdn_conv_w, gdn_a_log, gdn_dt_bias, gdn_norm_g, w_out,
                norm2_g, peer_w_q, peer_sub_keys, peer_u, peer_v):
    f32 = jnp.float32
    B, L, _ = x.shape
    xn = rms_norm(x, norm1_g)
    proj = xn @ w_in
    cuts = np.cumsum([DA_QK_W, DA_QK_W, DA_V_W, GDN_QKV, GDN_Z_W, GDN_HEADS]).tolist()
    q_da, k_da, v_da, qkv, z, b_raw, a_raw = jnp.split(proj, cuts, axis=-1)

    q = rms_norm(q_da.reshape(B, L, DA_HEADS, 2, DA_HD), da_q_g)
    k = rms_norm(k_da.reshape(B, L, DA_HEADS, 2, DA_HD), da_k_g)
    v = v_da.reshape(B, L, DA_HEADS, DA_VD)
    lam = (jnp.exp(jnp.sum(lambda_q1.astype(f32) * lambda_k1.astype(f32)))
           - jnp.exp(jnp.sum(lambda_q2.astype(f32) * lambda_k2.astype(f32))) + lambda_init)
    o_da = attend(q, k, v, lam)
    o_da = rms_norm(o_da, da_subln_g) * (1.0 - lambda_init)

    conv_in = jnp.concatenate([conv_buf.astype(x.dtype), qkv], axis=1)
    qkv_c = causal_conv_silu(conv_in, gdn_conv_w)
    qg, kg, vg = jnp.split(qkv_c, [GDN_HEADS * GDN_DK, 2 * GDN_HEADS * GDN_DK], axis=-1)
    qg = l2_norm(qg.reshape(B, L, GDN_HEADS, GDN_DK)) * (GDN_DK ** -0.5)
    kg = l2_norm(kg.reshape(B, L, GDN_HEADS, GDN_DK))
    vg = vg.reshape(B, L, GDN_HEADS, GDN_DV)
    beta = jax.nn.sigmoid(b_raw.astype(f32))
    g = -jnp.exp(gdn_a_log.astype(f32)) * jax.nn.softplus(a_raw.astype(f32) + gdn_dt_bias.astype(f32))
    o_g, new_ssm = gated_delta_rule(qg, kg, vg, beta, g, ssm_state)
    o_g = rms_norm(o_g, gdn_norm_g) * jax.nn.silu(z.reshape(B, L, GDN_HEADS, GDN_DV))

    mix = jnp.concatenate([o_da.reshape(B, L, DA_V_W), o_g.reshape(B, L, GDN_HEADS * GDN_DV)], axis=-1) @ w_out
    h = x + mix
    y = h + peer_ffn(rms_norm(h, norm2_g), peer_w_q, peer_sub_keys, peer_u, peer_v)
    return y, k, v, new_ssm, conv_in[:, -(CONV_W - 1):]


def setup_inputs(seed: int = 0) -> dict:
    key = jax.random.key(seed)
    ks = jax.random.split(key, 32)
    f32 = jnp.float32
    n_pages = PAST_LEN // PAGE_SIZE
    n_phys = (DEC_BATCH * n_pages * 5) // 4

    def nrm(k, shape, s):
        return jax.random.normal(k, shape, f32) * s

    def gain(k, shape):
        return 1.0 + 0.01 * jax.random.normal(k, shape, f32)

    dt = jnp.exp(jax.random.uniform(ks[14], (DEPTH, GDN_HEADS), f32, math.log(1e-3), math.log(0.1)))
    return {
        "x_prompt": nrm(ks[0], (BATCH, SEQ, D_MODEL), 1.0),
        "x_sample": nrm(ks[1], (DEC_BATCH, DEC_SEQ, D_MODEL), 1.0),
        "cache_k": nrm(ks[2], (DEPTH, n_phys, PAGE_SIZE, DA_HEADS, 2, DA_HD), 1.0),
        "cache_v": nrm(ks[3], (DEPTH, n_phys, PAGE_SIZE, DA_HEADS, DA_VD), 1.0),
        "state_ssm": nrm(ks[4], (DEPTH, DEC_BATCH, GDN_HEADS, GDN_DK, GDN_DV), 0.1),
        "state_conv": nrm(ks[5], (DEPTH, DEC_BATCH, CONV_W - 1, GDN_QKV), 1.0),
        "page_table": jax.random.permutation(ks[6], n_phys)[: DEC_BATCH * n_pages]
                        .reshape(DEC_BATCH, n_pages).astype(jnp.int32),
        "rel_bias": nrm(ks[7], (REL_BUCKETS, DA_HEADS), 0.5),
        "norm1_g": gain(ks[8], (DEPTH, D_MODEL)),
        "w_in": nrm(ks[9], (DEPTH, D_MODEL, IN_DIM), D_MODEL ** -0.5),
        "da_q_g": gain(ks[10], (DEPTH, DA_HD)),
        "da_k_g": gain(ks[11], (DEPTH, DA_HD)),
        "lambda_q1": nrm(ks[12], (DEPTH, DA_HD), 0.1),
        "lambda_k1": nrm(ks[13], (DEPTH, DA_HD), 0.1),
        "lambda_q2": nrm(ks[15], (DEPTH, DA_HD), 0.1),
        "lambda_k2": nrm(ks[16], (DEPTH, DA_HD), 0.1),
        "da_subln_g": gain(ks[17], (DEPTH, DA_VD)),
        "gdn_conv_w": nrm(ks[18], (DEPTH, CONV_W, GDN_QKV), 0.5),
        "gdn_a_log": jnp.log(jax.random.uniform(ks[19], (DEPTH, GDN_HEADS), f32, 1.0, 16.0)),
        "gdn_dt_bias": dt + jnp.log(-jnp.expm1(-dt)),
        "gdn_norm_g": gain(ks[20], (DEPTH, GDN_DV)),
        "w_out": nrm(ks[21], (DEPTH, MIX_W, D_MODEL), MIX_W ** -0.5),
        "norm2_g": gain(ks[22], (DEPTH, D_MODEL)),
        "peer_w_q": nrm(ks[23], (DEPTH, D_MODEL, PEER_HEADS * PEER_QDIM), D_MODEL ** -0.5),
        "peer_sub_keys": nrm(ks[24], (DEPTH, PEER_HEADS, 2, N_KEYS, PEER_QDIM // 2), (PEER_QDIM // 2) ** -0.5),
        "peer_u": nrm(ks[25], (DEPTH, N_EXPERTS, D_MODEL), D_MODEL ** -0.5),
        "peer_v": nrm(ks[26], (DEPTH, N_EXPERTS, D_MODEL), PEER_HEADS ** -0.5),
    }


def reference(x_prompt, x_sample, cache_k, cache_v, state_ssm, state_conv, page_table, rel_bias,
              norm1_g, w_in, da_q_g, da_k_g, lambda_q1, lambda_k1, lambda_q2, lambda_k2, da_subln_g,
              gdn_conv_w, gdn_a_log, gdn_dt_bias, gdn_norm_g, w_out,
              norm2_g, peer_w_q, peer_sub_keys, peer_u, peer_v):
    n_dec, n_pages = page_table.shape
    past = n_pages * PAGE_SIZE
    bp = x_prompt.shape[0]
    yp, ys = x_prompt, x_sample
    kp_l, vp_l, sp_l, cp_l, ks_l, vs_l, ss_l, cs_l = [], [], [], [], [], [], [], []
    layer_params = (norm1_g, w_in, da_q_g, da_k_g, lambda_q1, lambda_k1, lambda_q2, lambda_k2, da_subln_g,
                    gdn_conv_w, gdn_a_log, gdn_dt_bias, gdn_norm_g, w_out,
                    norm2_g, peer_w_q, peer_sub_keys, peer_u, peer_v)
    for l in range(DEPTH):
        lambda_init = 0.8 - 0.6 * math.exp(-0.3 * l)
        lp = [t[l] for t in layer_params]
        zero_conv = jnp.zeros((bp, CONV_W - 1, GDN_QKV), x_prompt.dtype)
        zero_ssm = jnp.zeros((bp, GDN_HEADS, GDN_DK, GDN_DV), x_prompt.dtype)
        yp, kp, vp, sp, cp = trunk_layer(yp, zero_conv, zero_ssm,
                                         functools.partial(attend_prompt, rel_bias=rel_bias),
                                         lambda_init, *lp)
        k_past = cache_k[l, page_table].reshape(n_dec, past, DA_HEADS, 2, DA_HD)
        v_past = cache_v[l, page_table].reshape(n_dec, past, DA_HEADS, DA_VD)
        ys, ks, vs, ss, cs = trunk_layer(ys, state_conv[l], state_ssm[l],
                                         functools.partial(attend_with_past, k_past=k_past, v_past=v_past,
                                                           rel_bias=rel_bias),
                                         lambda_init, *lp)
        kp_l.append(kp); vp_l.append(vp); sp_l.append(sp); cp_l.append(cp)
        ks_l.append(ks); vs_l.append(vs); ss_l.append(ss); cs_l.append(cs)
    return (yp, ys,
            jnp.stack(kp_l), jnp.stack(vp_l), jnp.stack(sp_l), jnp.stack(cp_l),
            jnp.stack(ks_l), jnp.stack(vs_l), jnp.stack(ss_l), jnp.stack(cs_l))
```

```python
import functools
import math

import numpy as np
import jax
import jax.numpy as jnp
from jax import lax
from jax.experimental import pallas as pl
from jax.experimental.pallas import tpu as pltpu

F32 = jnp.float32
BF16 = jnp.bfloat16
HI = lax.Precision.HIGHEST

EPS = 1e-6
NEG = -1e30
PAGE_SIZE = 128
DA_HEADS = 4
DA_HD = 64
DA_VD = 2 * DA_HD
GDN_HEADS = 4
GDN_DK = 128
GDN_DV = 128
CONV_W = 4
GDN_CHUNK = 64
REL_BUCKETS = 32
REL_MAX_DIST = 128
PEER_HEADS = 8
N_KEYS = 128
PEER_TOPK = 16
LANES = 128
SUBLANES = 8

PROJ_TM = 256
ATTN_TQ = 512
PAGES_PER_STEP = 8
GDN_TB = 128
PEER_TB = 512
PEER_EC = 1024
VMEM_LIMIT = 56 * 1024 * 1024


def _dot(a, b, prec=None):
    return jnp.dot(a, b, preferred_element_type=F32, precision=prec)


def _dot_nt(a, b, prec=None):
    return lax.dot_general(a, b, (((1,), (1,)), ((), ())), preferred_element_type=F32, precision=prec)


def _params(n_axes, vmem=VMEM_LIMIT):
    return pltpu.CompilerParams(dimension_semantics=("arbitrary",) * n_axes, vmem_limit_bytes=vmem)


def _bucket_lower_bounds():
    n = np.arange(REL_MAX_DIST + 1)
    max_exact = REL_BUCKETS // 2
    nf = np.maximum(n, 1).astype(np.float32)
    large = max_exact + (np.log(nf / max_exact) / math.log(REL_MAX_DIST / max_exact)
                         * (REL_BUCKETS - max_exact)).astype(np.int32)
    bucket = np.where(n < max_exact, n, np.minimum(large, REL_BUCKETS - 1))
    assert bucket[-1] == REL_BUCKETS - 1 and np.all(np.diff(bucket) >= 0)
    return [int(np.argmax(bucket >= b)) for b in range(REL_BUCKETS)]


_BUCKET_LO = _bucket_lower_bounds()


def _bias_from_dist(dist, bias_of_bucket):
    val = bias_of_bucket(REL_BUCKETS - 1)
    for b in range(REL_BUCKETS - 2, -1, -1):
        val = jnp.where(dist < _BUCKET_LO[b + 1], bias_of_bucket(b), val)
    return val


def _group_sum(x2, grp):
    hi = x2.astype(BF16)
    lo = (x2 - hi.astype(F32)).astype(BF16)
    return _dot(hi, grp) + _dot(lo, grp)


def _lambda(lam_ref, lambda_init):
    lv = lam_ref[...]
    t1 = jnp.sum(lv[0:1] * lv[1:2], axis=-1, keepdims=True)
    t2 = jnp.sum(lv[2:3] * lv[3:4], axis=-1, keepdims=True)
    return jnp.exp(t1) - jnp.exp(t2) + lambda_init


def _softmax_update(m, l, acc, s, v):
    m_new = jnp.maximum(m, jnp.max(s, axis=-1, keepdims=True))
    alpha = jnp.exp(m - m_new)
    p = jnp.exp(s - m_new)
    l_new = alpha * l + jnp.sum(p, axis=-1, keepdims=True)
    acc_new = alpha * acc + _dot(p.astype(BF16), v)
    return m_new, l_new, acc_new


def _inproj_kernel(x_ref, g1_ref, wqk_ref, wv_ref, wg_ref, wz_ref, wba_ref, gq_ref, gk_ref, grp_ref,
                   q_ref, k_ref, v_ref, qkv_ref, z_ref, ba_ref):
    x = x_ref[...]
    ms = jnp.mean(x * x, axis=-1, keepdims=True)
    xn = (x * lax.rsqrt(ms + EPS) * g1_ref[...]).astype(BF16)
    qk = _dot(xn, wqk_ref[...])
    half = qk.shape[1] // 2
    grp = grp_ref[...]

    def qk_norm(t, g):
        ms_g = _group_sum(t * t, grp) * (1.0 / DA_HD)
        return t * lax.rsqrt(ms_g + EPS) * g

    q_ref[...] = qk_norm(qk[:, :half], gq_ref[...]) * (DA_HD ** -0.5)
    k_ref[...] = qk_norm(qk[:, half:], gk_ref[...])
    v_ref[...] = _dot(xn, wv_ref[...])
    qkv_ref[...] = _dot(xn, wg_ref[...])
    z_ref[...] = _dot(xn, wz_ref[...])
    ba_ref[...] = _dot(xn, wba_ref[...])


def _inproj(x, g1, wqk, wv, wg, wz, wba, gq, gk, grp):
    t, d = x.shape
    tm = min(PROJ_TM, t)
    row = lambda i: (i, 0)
    fixed = lambda i: (0, 0)
    widths = (wqk.shape[1] // 2, wqk.shape[1] // 2, wv.shape[1], wg.shape[1], wz.shape[1], wba.shape[1])
    return pl.pallas_call(
        _inproj_kernel,
        grid=(t // tm,),
        in_specs=[pl.BlockSpec((tm, d), row), pl.BlockSpec(g1.shape, fixed)]
        + [pl.BlockSpec(w.shape, fixed) for w in (wqk, wv, wg, wz, wba, gq, gk, grp)],
        out_specs=[pl.BlockSpec((tm, w), row) for w in widths],
        out_shape=[jax.ShapeDtypeStruct((t, w), F32) for w in widths],
        compiler_params=_params(1),
    )(x, g1, wqk, wv, wg, wz, wba, gq, gk, grp)


def _attn_prompt_kernel(rb_ref, q_ref, k_ref, v_ref, lam_ref, sg_ref, o_ref, bias_scr, kb_scr, vb_scr,
                        *, tq, lambda_init):
    h = pl.program_id(0)
    b = pl.program_id(1)
    qi = pl.program_id(2)

    @pl.when((b == 0) & (qi == 0))
    def _():
        r = lax.broadcasted_iota(jnp.int32, (tq, tq), 0)
        c = lax.broadcasted_iota(jnp.int32, (tq, tq), 1)
        for which in range(2):
            dist = r - c + which * tq
            val = _bias_from_dist(dist, lambda bk: rb_ref[bk, h])
            bias_scr[which] = jnp.where(dist >= 0, val, NEG)

    @pl.when(qi == 0)
    def _():
        kb_scr[...] = k_ref[...].astype(BF16)
        vb_scr[...] = v_ref[...].astype(BF16)

    q = q_ref[...]
    lane = lax.broadcasted_iota(jnp.int32, q.shape, 1)
    qs = (jnp.where(lane < DA_HD, q, 0.0).astype(BF16), jnp.where(lane >= DA_HD, q, 0.0).astype(BF16))
    far_bias = rb_ref[REL_BUCKETS - 1, h]

    def block(j, bias, carry):
        start = pl.multiple_of(j * tq, tq)
        kblk = kb_scr[pl.ds(start, tq), :]
        vblk = vb_scr[pl.ds(start, tq), :]
        return tuple(_softmax_update(*carry[i], _dot_nt(qs[i], kblk) + bias, vblk) for i in range(2))

    init = (jnp.full((tq, 1), NEG, F32), jnp.zeros((tq, 1), F32), jnp.zeros((tq, DA_VD), F32))
    carry = block(qi, bias_scr[0], (init, init))
    carry = block(jnp.maximum(qi - 1, 0), bias_scr[1] + jnp.where(qi == 0, NEG, 0.0), carry)
    carry = lax.fori_loop(0, jnp.maximum(qi - 1, 0), lambda j, cr: block(j, far_bias, cr), carry)

    (_, l1, acc1), (_, l2, acc2) = carry
    o = acc1 * (1.0 / l1) - _lambda(lam_ref, lambda_init) * (acc2 * (1.0 / l2))
    o = o * lax.rsqrt(jnp.mean(o * o, axis=-1, keepdims=True) + EPS) * sg_ref[...] * (1.0 - lambda_init)
    o_ref[...] = o.astype(o_ref.dtype)


def _attn_prompt(q, k, v, rel_bias, lamv, subln_g, n_batch, seq, lambda_init):
    tq = min(ATTN_TQ, seq)
    nq = seq // tq
    kern = functools.partial(_attn_prompt_kernel, tq=tq, lambda_init=lambda_init)
    kv_spec = pl.BlockSpec((seq, DA_VD), lambda h, b, i: (b, h))
    q_spec = pl.BlockSpec((tq, DA_VD), lambda h, b, i: (b * nq + i, h))
    return pl.pallas_call(
        kern,
        grid=(DA_HEADS, n_batch, nq),
        in_specs=[pl.BlockSpec(memory_space=pltpu.SMEM), q_spec, kv_spec, kv_spec,
                  pl.BlockSpec(lamv.shape, lambda h, b, i: (0, 0)),
                  pl.BlockSpec(subln_g.shape, lambda h, b, i: (0, 0))],
        out_specs=q_spec,
        out_shape=jax.ShapeDtypeStruct(q.shape, BF16),
        scratch_shapes=[pltpu.VMEM((2, tq, tq), F32), pltpu.VMEM((seq, DA_VD), BF16),
                        pltpu.VMEM((seq, DA_VD), BF16)],
        compiler_params=_params(3),
    )(rel_bias, q, k, v, lamv, subln_g)


def _attn_sample_kernel(pt_ref, q_ref, kn_ref, vn_ref, rbt_ref, lam_ref, sg_ref, *rest,
                        pps, n_new, lambda_init):
    del pt_ref
    k_pages = rest[:pps]
    v_pages = rest[pps:2 * pps]
    o_ref, kcat, vcat, m_scr, l_scr, acc_scr = rest[2 * pps:]
    p = pl.program_id(1)
    last = pl.num_programs(1) - 1
    rows = 2 * DA_HEADS * SUBLANES
    ps = PAGE_SIZE

    @pl.when(p == 0)
    def _():
        m_scr[...] = jnp.full(m_scr.shape, NEG, F32)
        l_scr[...] = jnp.zeros(l_scr.shape, F32)
        acc_scr[...] = jnp.zeros(acc_scr.shape, F32)

    qblk = q_ref[...].astype(BF16)
    rbt = rbt_ref[...]
    far_bias = rbt[:, REL_BUCKETS - 1:REL_BUCKETS]

    for i in range(pps):
        kcat[i * ps:(i + 1) * ps, :] = k_pages[i][...].astype(BF16)
        vcat[i * ps:(i + 1) * ps, :] = v_pages[i][...].astype(BF16)
    s = _dot_nt(qblk, kcat[...])
    tok = lax.broadcasted_iota(jnp.int32, (rows, ps), 0) % SUBLANES
    off = lax.broadcasted_iota(jnp.int32, (rows, ps), 1)
    near = _bias_from_dist(ps + tok - off, lambda bk: rbt[:, bk:bk + 1])
    tail_bias = jnp.where(p == last, near, far_bias)
    split = (pps - 1) * ps
    s = jnp.concatenate([s[:, :split] + far_bias, s[:, split:] + tail_bias], axis=1)
    m, l, acc = _softmax_update(m_scr[...], l_scr[...], acc_scr[...], s, vcat[...])
    m_scr[...] = m
    l_scr[...] = l
    acc_scr[...] = acc

    @pl.when(p == last)
    def _():
        qf = qblk.astype(F32)
        kn = kn_ref[...].astype(BF16).astype(F32)
        vn = vn_ref[...].astype(BF16).astype(F32)
        tok_c = tok[:, 0:1]
        cols = []
        for tp in range(n_new):
            sc = jnp.sum(qf * kn[tp:tp + 1], axis=-1, keepdims=True)
            dist = tok_c - tp
            bias = rbt[:, n_new - 1:n_new]
            for dd in range(n_new - 2, -1, -1):
                bias = jnp.where(dist == dd, rbt[:, dd:dd + 1], bias)
            cols.append(jnp.where(dist >= 0, sc + bias, NEG))
        m_old = m_scr[...]
        m_new = _tree(jnp.maximum, [m_old] + cols)
        alpha = jnp.exp(m_old - m_new)
        l2 = alpha * l_scr[...]
        acc2 = alpha * acc_scr[...]
        for tp in range(n_new):
            pw = jnp.exp(cols[tp] - m_new)
            l2 = l2 + pw
            acc2 = acc2 + pw.astype(BF16).astype(F32) * vn[tp:tp + 1]
        a = acc2 * (1.0 / l2)
        lam = _lambda(lam_ref, lambda_init)
        outs = []
        for hh in range(DA_HEADS):
            r0 = hh * 2 * SUBLANES
            cols_h = slice(hh * DA_VD, (hh + 1) * DA_VD)
            o = a[r0:r0 + SUBLANES, cols_h] - lam * a[r0 + SUBLANES:r0 + 2 * SUBLANES, cols_h]
            o = o * lax.rsqrt(jnp.mean(o * o, axis=-1, keepdims=True) + EPS) * sg_ref[...] * (1.0 - lambda_init)
            outs.append(o)
        o_ref[...] = jnp.concatenate(outs, axis=1)


def _attn_sample(q3, k3, v3, cache_k2, cache_v2, page_table, layer_base, rbt, lamv, subln_g, lambda_init):
    n_dec, n_new, width = k3.shape
    n_pages = page_table.shape[1]
    pps = math.gcd(PAGES_PER_STEP, n_pages)
    assert n_new <= SUBLANES <= _BUCKET_LO[REL_BUCKETS // 2]
    rows = 2 * DA_HEADS * SUBLANES
    kern = functools.partial(_attn_sample_kernel, pps=pps, n_new=n_new, lambda_init=lambda_init)
    tok_spec = pl.BlockSpec((None, n_new, width), lambda b, p, pt: (b, 0, 0))
    q_spec = pl.BlockSpec((None, rows, width), lambda b, p, pt: (b, 0, 0))
    o_spec = pl.BlockSpec((None, SUBLANES, width), lambda b, p, pt: (b, 0, 0))
    fixed = lambda shape: pl.BlockSpec(shape, lambda b, p, pt: (0,) * len(shape))

    def page_spec(i):
        return pl.BlockSpec((None, PAGE_SIZE, width),
                            lambda b, p, pt: (layer_base + pt[b * n_pages + p * pps + i], 0, 0))

    grid_spec = pltpu.PrefetchScalarGridSpec(
        num_scalar_prefetch=1,
        grid=(n_dec, n_pages // pps),
        in_specs=[q_spec, tok_spec, tok_spec, fixed(rbt.shape), fixed(lamv.shape), fixed(subln_g.shape)]
        + [page_spec(i) for i in range(pps)] * 2,
        out_specs=o_spec,
        scratch_shapes=[pltpu.VMEM((pps * PAGE_SIZE, width), BF16), pltpu.VMEM((pps * PAGE_SIZE, width), BF16),
                        pltpu.VMEM((rows, 1), F32), pltpu.VMEM((rows, 1), F32), pltpu.VMEM((rows, width), F32)],
    )
    return pl.pallas_call(
        kern, grid_spec=grid_spec, out_shape=jax.ShapeDtypeStruct((n_dec, SUBLANES, width), F32),
        compiler_params=_params(2),
    )(page_table.reshape(-1), q3, k3, v3, rbt, lamv, subln_g, *([cache_k2] * pps), *([cache_v2] * pps))


def _gdn_kernel(qkv_ref, z_ref, ba_ref, conv0_ref, s0_ref, cw_ref, alog_ref, dtb_ref, gn_ref, tri_ref,
                o_ref, s_ref, ext_scr, *, tb, valid_len):
    blk = pl.program_id(1)
    hk = GDN_HEADS * GDN_DK
    c_len = GDN_CHUNK

    @pl.when(blk == 0)
    def _():
        ext_scr[0:SUBLANES, :] = conv0_ref[...]
        s_ref[...] = s0_ref[...]

    ext_scr[SUBLANES:SUBLANES + tb, :] = qkv_ref[...]
    cw = cw_ref[...]
    y = ext_scr[SUBLANES:SUBLANES + tb, :] * cw[CONV_W - 1:CONV_W]
    for j in range(1, CONV_W):
        y = y + ext_scr[SUBLANES - j:SUBLANES - j + tb, :] * cw[CONV_W - 1 - j:CONV_W - j]
    ext_scr[0:SUBLANES, :] = ext_scr[tb:tb + SUBLANES, :]
    y = y * jax.nn.sigmoid(y)

    ba = ba_ref[...]
    beta_all = jax.nn.sigmoid(ba)
    sp_in = ba + dtb_ref[...]
    softplus = jnp.maximum(sp_in, 0.0) + jnp.log(1.0 + jnp.exp(-jnp.abs(sp_in)))
    g_all = -jnp.exp(alog_ref[...]) * softplus
    if valid_len < tb:
        ok = lax.broadcasted_iota(jnp.int32, (tb, 1), 0) < valid_len
        y = jnp.where(ok, y, 0.0)
        beta_all = jnp.where(ok, beta_all, 0.0)
        g_all = jnp.where(ok, g_all, 0.0)

    ii = lax.broadcasted_iota(jnp.int32, (c_len, c_len), 0)
    jj = lax.broadcasted_iota(jnp.int32, (c_len, c_len), 1)
    eye = (ii == jj).astype(F32)
    z = z_ref[...]
    for c in range(tb // c_len):
        rs = slice(c * c_len, (c + 1) * c_len)
        gc_all = _dot(tri_ref[...], g_all[rs], HI)
        gc_all_t = gc_all.T
        for h in range(GDN_HEADS):
            cs = slice(h * GDN_DK, (h + 1) * GDN_DK)
            q = y[rs, h * GDN_DK:(h + 1) * GDN_DK]
            k = y[rs, hk + h * GDN_DK:hk + (h + 1) * GDN_DK]
            v = y[rs, 2 * hk + h * GDN_DV:2 * hk + (h + 1) * GDN_DV]
            q = q * lax.rsqrt(jnp.sum(q * q, axis=-1, keepdims=True) + EPS) * (GDN_DK ** -0.5)
            k = k * lax.rsqrt(jnp.sum(k * k, axis=-1, keepdims=True) + EPS)
            beta = beta_all[rs, h:h + 1]
            gc = gc_all[:, GDN_HEADS + h:GDN_HEADS + h + 1]
            gr = gc_all_t[GDN_HEADS + h:GDN_HEADS + h + 1, :]
            decay = jnp.where(ii >= jj, jnp.exp(jnp.minimum(gc - gr, 0.0)), 0.0)
            kb = k * beta
            a_mat = jnp.where(ii > jj, _dot_nt(kb, k, HI) * decay, 0.0)
            x = -a_mat
            t_inv = eye + x
            for _ in range(int(math.log2(c_len)) - 1):
                x = _dot(x, x, HI)
                t_inv = t_inv + _dot(t_inv, x, HI)
            egc = jnp.exp(gc)
            sol = _dot(t_inv, jnp.concatenate([v * beta, kb * egc], axis=1), HI)
            u, w = sol[:, :GDN_DV], sol[:, GDN_DV:]
            qk = _dot_nt(q, k, HI) * decay
            state = s_ref[h]
            v_new = u - _dot(w, state, HI)
            o = _dot(q * egc, state, HI) + _dot(qk, v_new, HI)
            g_last = gc[c_len - 1:c_len, :]
            kd = k * jnp.exp(g_last - gc)
            s_ref[h] = state * jnp.exp(g_last) + _dot(kd.T, v_new, HI)
            o = o * lax.rsqrt(jnp.mean(o * o, axis=-1, keepdims=True) + EPS) * gn_ref[...]
            zz = z[rs, cs]
            o_ref[rs, cs] = (o * (zz * jax.nn.sigmoid(zz))).astype(o_ref.dtype)


def _gdn(qkv, z, ba, conv0, s0, cw, alog, dtb, gn, tri, n_batch, seq_pad, valid_len):
    tb = min(GDN_TB, seq_pad)
    nblk = seq_pad // tb
    assert valid_len == seq_pad or nblk == 1
    kern = functools.partial(_gdn_kernel, tb=tb, valid_len=min(valid_len, tb))
    row = lambda w: pl.BlockSpec((tb, w), lambda b, i: (b * nblk + i, 0))
    fixed = lambda a: pl.BlockSpec(a.shape, lambda b, i: (0,) * a.ndim)
    state_spec = pl.BlockSpec((None,) + s0.shape[1:], lambda b, i: (b, 0, 0, 0))
    return pl.pallas_call(
        kern,
        grid=(n_batch, nblk),
        in_specs=[row(qkv.shape[1]), row(z.shape[1]), row(ba.shape[1]),
                  pl.BlockSpec((None,) + conv0.shape[1:], lambda b, i: (b, 0, 0)), state_spec,
                  fixed(cw), fixed(alog), fixed(dtb), fixed(gn), fixed(tri)],
        out_specs=[row(z.shape[1]), state_spec],
        out_shape=[jax.ShapeDtypeStruct(z.shape, BF16), jax.ShapeDtypeStruct(s0.shape, F32)],
        scratch_shapes=[pltpu.VMEM((tb + SUBLANES, qkv.shape[1]), F32)],
        compiler_params=_params(2),
    )(qkv, z, ba, conv0, s0, cw, alog, dtb, gn, tri)


def _outproj_kernel(x_ref, oda_ref, og_ref, wo1_ref, wo2_ref, g2_ref, wq_ref, sk_ref, h_ref, hnt_ref, st_ref):
    hres = x_ref[...] + _dot(oda_ref[...].astype(BF16), wo1_ref[...]) + _dot(og_ref[...], wo2_ref[...])
    h_ref[...] = hres
    ms = jnp.mean(hres * hres, axis=-1, keepdims=True)
    hn = hres * lax.rsqrt(ms + EPS) * g2_ref[...]
    hnt_ref[...] = hn.T.astype(BF16)
    q = _dot(hn.astype(BF16), wq_ref[...]).astype(BF16)
    for hs in range(sk_ref.shape[0]):
        st_ref[hs] = _dot_nt(sk_ref[hs], q[:, hs * LANES:(hs + 1) * LANES])


def _outproj(x, oda, og, wo1, wo2, g2, wq, sk):
    t, d = x.shape
    tm = min(PROJ_TM, t)
    row = lambda w: pl.BlockSpec((tm, w), lambda i: (i, 0))
    fixed = lambda a: pl.BlockSpec(a.shape, lambda i: (0,) * a.ndim)
    return pl.pallas_call(
        _outproj_kernel,
        grid=(t // tm,),
        in_specs=[row(d), row(oda.shape[1]), row(og.shape[1])] + [fixed(a) for a in (wo1, wo2, g2, wq, sk)],
        out_specs=[row(d), pl.BlockSpec((d, tm), lambda i: (0, i)),
                   pl.BlockSpec((sk.shape[0], N_KEYS, tm), lambda i: (0, 0, i))],
        out_shape=[jax.ShapeDtypeStruct((t, d), F32), jax.ShapeDtypeStruct((d, t), BF16),
                   jax.ShapeDtypeStruct((sk.shape[0], N_KEYS, t), F32)],
        compiler_params=_params(1),
    )(x, oda, og, wo1, wo2, g2, wq, sk)


def _bitonic_merge_desc(lst):
    n = len(lst)
    j = n // 2
    while j >= 1:
        for i in range(n):
            l = i ^ j
            if l > i:
                lst[i], lst[l] = jnp.maximum(lst[i], lst[l]), jnp.minimum(lst[i], lst[l])
        j //= 2
    return lst


def _bitonic_sort_desc(lst):
    n = len(lst)
    k = 2
    while k <= n:
        j = k // 2
        while j >= 1:
            for i in range(n):
                l = i ^ j
                if l > i:
                    hi, lo = jnp.maximum(lst[i], lst[l]), jnp.minimum(lst[i], lst[l])
                    lst[i], lst[l] = (hi, lo) if (i & k) == 0 else (lo, hi)
            j //= 2
        k *= 2
    return lst


def _tree(fn, xs):
    xs = list(xs)
    while len(xs) > 1:
        xs = [fn(xs[i], xs[i + 1]) if i + 1 < len(xs) else xs[i] for i in range(0, len(xs), 2)]
    return xs[0]


def _peer_topk_kernel(st_ref, ap_ref, bp_ref, cp_ref):
    kk = PEER_TOPK
    n_vr = N_KEYS // SUBLANES
    tops = []
    for hs in range(2 * PEER_HEADS):
        x = st_ref[hs]
        lst = _bitonic_sort_desc([x[j * SUBLANES:(j + 1) * SUBLANES, :] for j in range(n_vr)])[:kk]
        shift = SUBLANES // 2
        while shift >= 1:
            other = [pltpu.roll(a, shift, 0) for a in lst]
            lst = _bitonic_merge_desc([jnp.maximum(lst[i], other[kk - 1 - i]) for i in range(kk)])
            shift //= 2
        tops.append(lst)

    sub = lax.broadcasted_iota(jnp.int32, tops[0][0].shape, 0)

    def pack(which, a):
        out = tops[which][a]
        for h in range(1, PEER_HEADS):
            out = jnp.where(sub == h, tops[2 * h + which][a], out)
        return out

    v1 = [pack(0, a) for a in range(kk)]
    v2 = [pack(1, a) for a in range(kk)]
    pairs = [(a, b) for a in range(kk) for b in range(kk) if (a + 1) * (b + 1) <= kk]
    cands = [v1[a] + v2[b] for a, b in pairs]
    cur = list(cands)
    for r in range(kk):
        tau = _tree(jnp.maximum, cur)
        if r + 1 < kk:
            cur = [jnp.where(c == tau, -jnp.inf, c) for c in cur]
    top = v1[0] + v2[0]
    zsum = _tree(jnp.add, [jnp.where(c >= tau, jnp.exp(c - top), 0.0) for c in cands])
    rz = 1.0 / zsum
    a_top = [jnp.exp(v - v1[0]) * rz for v in v1]
    b_top = [jnp.exp(v - v2[0]) for v in v2]
    cp_ref[...] = _tree(jnp.minimum, [jnp.where(c >= tau, a_top[a] * b_top[b], jnp.inf)
                                      for c, (a, b) in zip(cands, pairs)])
    for h in range(PEER_HEADS):
        ap_ref[h] = jnp.exp(st_ref[2 * h] - v1[0][h:h + 1, :]) * rz[h:h + 1, :]
        bp_ref[h] = jnp.exp(st_ref[2 * h + 1] - v2[0][h:h + 1, :])


def _peer_topk(st):
    n_hs, n_keys, t = st.shape
    blk = pl.BlockSpec((PEER_HEADS, n_keys, LANES), lambda i: (0, 0, i))
    return pl.pallas_call(
        _peer_topk_kernel,
        grid=(t // LANES,),
        in_specs=[pl.BlockSpec((n_hs, n_keys, LANES), lambda i: (0, 0, i))],
        out_specs=[blk, blk, pl.BlockSpec((PEER_HEADS, LANES), lambda i: (0, i))],
        out_shape=[jax.ShapeDtypeStruct((PEER_HEADS, n_keys, t), F32)] * 2
        + [jax.ShapeDtypeStruct((PEER_HEADS, t), F32)],
        compiler_params=_params(1),
    )(st)


def _gelu_tanh(x):
    return 0.5 * x * (1.0 + jnp.tanh(math.sqrt(2.0 / math.pi) * (x + 0.044715 * (x * x * x))))


def _peer_dense_kernel(hnt_ref, ap_ref, bp_ref, cp_ref, u_ref, vt_ref, h_ref, y_ref, acc_scr, w_scr, *, tb, ec):
    e = pl.program_id(1)
    rows_per_step = ec // N_KEYS

    @pl.when(e == 0)
    def _():
        acc_scr[...] = jnp.zeros(acc_scr.shape, F32)

    hnt = hnt_ref[...]
    for i in range(rows_per_step):
        hid = _dot(u_ref[i * N_KEYS:(i + 1) * N_KEYS, :], hnt)
        for tl in range(tb // LANES):
            ts = slice(tl * LANES, (tl + 1) * LANES)
            gate = jnp.zeros((N_KEYS, LANES), F32)
            for h in range(PEER_HEADS):
                prod = ap_ref[h, i:i + 1, ts] * bp_ref[h, :, ts]
                gate = gate + jnp.where(prod >= cp_ref[h:h + 1, ts], prod, 0.0)
            w_scr[i * N_KEYS:(i + 1) * N_KEYS, ts] = (gate * _gelu_tanh(hid[:, ts])).astype(BF16)
    acc_scr[...] += _dot(vt_ref[...], w_scr[...])

    @pl.when(e == pl.num_programs(1) - 1)
    def _():
        y_ref[...] = h_ref[...] + acc_scr[...].T


def _peer_dense(hnt, ap, bp, cp, u, vt, hres):
    d, t = hnt.shape
    n_exp = u.shape[0]
    tb = min(PEER_TB, t)
    ec = PEER_EC
    kern = functools.partial(_peer_dense_kernel, tb=tb, ec=ec)
    tok3 = pl.BlockSpec((PEER_HEADS, N_KEYS, tb), lambda i, e: (0, 0, i))
    return pl.pallas_call(
        kern,
        grid=(t // tb, n_exp // ec),
        in_specs=[pl.BlockSpec((d, tb), lambda i, e: (0, i)),
                  pl.BlockSpec((PEER_HEADS, ec // N_KEYS, tb), lambda i, e: (0, e, i)), tok3,
                  pl.BlockSpec((PEER_HEADS, tb), lambda i, e: (0, i)),
                  pl.BlockSpec((ec, d), lambda i, e: (e, 0)), pl.BlockSpec((d, ec), lambda i, e: (0, e)),
                  pl.BlockSpec((tb, d), lambda i, e: (i, 0))],
        out_specs=pl.BlockSpec((tb, d), lambda i, e: (i, 0)),
        out_shape=jax.ShapeDtypeStruct((t, d), F32),
        scratch_shapes=[pltpu.VMEM((d, tb), F32), pltpu.VMEM((ec, tb), BF16)],
        compiler_params=_params(2),
    )(hnt, ap, bp, cp, u, vt, hres)


def _group_matrix(width, group):
    idx = np.arange(width) // group
    return jnp.asarray(idx[:, None] == idx[None, :], dtype=BF16)


def _pad_lanes(vec, offset, width=LANES):
    return jnp.zeros((1, width), F32).at[0, offset:offset + vec.shape[0]].set(vec)


def _layer_weights(l, p):
    d = p["w_in"].shape[1]
    da_qk = DA_HEADS * 2 * DA_HD
    da_v = DA_HEADS * DA_VD
    g_qkv = GDN_HEADS * (2 * GDN_DK + GDN_DV)
    g_z = GDN_HEADS * GDN_DV
    cuts = np.cumsum([0, da_qk, da_qk, da_v, g_qkv, g_z, GDN_HEADS, GDN_HEADS]).tolist()
    w_in = p["w_in"][l].astype(BF16)
    wba = jnp.zeros((d, LANES), BF16).at[:, :2 * GDN_HEADS].set(w_in[:, cuts[5]:cuts[7]])
    w_out = p["w_out"][l].astype(BF16)
    return dict(
        g1=p["norm1_g"][l][None, :], wqk=w_in[:, cuts[0]:cuts[2]], wv=w_in[:, cuts[2]:cuts[3]],
        wg=w_in[:, cuts[3]:cuts[4]], wz=w_in[:, cuts[4]:cuts[5]], wba=wba,
        gq=jnp.tile(p["da_q_g"][l], 2 * DA_HEADS)[None, :], gk=jnp.tile(p["da_k_g"][l], 2 * DA_HEADS)[None, :],
        grp=_group_matrix(da_qk, DA_HD),
        lamv=jnp.stack([p["lambda_q1"][l], p["lambda_k1"][l], p["lambda_q2"][l], p["lambda_k2"][l]]),
        subln=p["da_subln_g"][l][None, :],
        cw=jnp.zeros((SUBLANES, g_qkv), F32).at[:CONV_W].set(p["gdn_conv_w"][l]),
        alog=_pad_lanes(p["gdn_a_log"][l], GDN_HEADS), dtb=_pad_lanes(p["gdn_dt_bias"][l], GDN_HEADS),
        gn=p["gdn_norm_g"][l][None, :],
        tri=jnp.asarray(np.tril(np.ones((GDN_CHUNK, GDN_CHUNK), np.float32))),
        wo1=w_out[:da_v], wo2=w_out[da_v:], g2=p["norm2_g"][l][None, :],
        wq=p["peer_w_q"][l].astype(BF16),
        sk=p["peer_sub_keys"][l].reshape(2 * PEER_HEADS, N_KEYS, -1).astype(BF16),
        u=p["peer_u"][l].astype(BF16), vt=p["peer_v"][l].T.astype(BF16),
    )


def _mixers_to_output(x2, oda, og, w):
    hres, hnt, st = _outproj(x2, oda, og, w["wo1"], w["wo2"], w["g2"], w["wq"], w["sk"])
    ap, bp, cp = _peer_topk(st)
    return _peer_dense(hnt, ap, bp, cp, w["u"], w["vt"], hres)


def kernel(x_prompt, x_sample, cache_k, cache_v, state_ssm, state_conv, page_table, rel_bias, norm1_g, w_in,
           da_q_g, da_k_g, lambda_q1, lambda_k1, lambda_q2, lambda_k2, da_subln_g, gdn_conv_w, gdn_a_log,
           gdn_dt_bias, gdn_norm_g, w_out, norm2_g, peer_w_q, peer_sub_keys, peer_u, peer_v):
    p = dict(norm1_g=norm1_g, w_in=w_in, da_q_g=da_q_g, da_k_g=da_k_g, lambda_q1=lambda_q1, lambda_k1=lambda_k1,
             lambda_q2=lambda_q2, lambda_k2=lambda_k2, da_subln_g=da_subln_g, gdn_conv_w=gdn_conv_w,
             gdn_a_log=gdn_a_log, gdn_dt_bias=gdn_dt_bias, gdn_norm_g=gdn_norm_g, w_out=w_out, norm2_g=norm2_g,
             peer_w_q=peer_w_q, peer_sub_keys=peer_sub_keys, peer_u=peer_u, peer_v=peer_v)
    depth = w_in.shape[0]
    bp_, seq, d = x_prompt.shape
    bd, n_new, _ = x_sample.shape
    n_phys = cache_k.shape[1]
    width = DA_HEADS * DA_VD
    g_qkv = GDN_HEADS * (2 * GDN_DK + GDN_DV)
    cache_k2 = cache_k.reshape(depth * n_phys, PAGE_SIZE, width)
    cache_v2 = cache_v.reshape(depth * n_phys, PAGE_SIZE, width)
    rbt = jnp.repeat(rel_bias.T, 2 * SUBLANES, axis=0)
    pad_s = -(-n_new // GDN_CHUNK) * GDN_CHUNK
    zero_conv = jnp.zeros((bp_, SUBLANES, g_qkv), F32)
    zero_ssm = jnp.zeros((bp_,) + state_ssm.shape[2:], F32)

    yp = x_prompt.reshape(bp_ * seq, d)
    ys = x_sample.reshape(bd * n_new, d)
    outs = [[] for _ in range(8)]
    for l in range(depth):
        lambda_init = 0.8 - 0.6 * math.exp(-0.3 * l)
        w = _layer_weights(l, p)
        proj_w = (w["g1"], w["wqk"], w["wv"], w["wg"], w["wz"], w["wba"], w["gq"], w["gk"], w["grp"])
        gdn_w = (w["cw"], w["alog"], w["dtb"], w["gn"], w["tri"])

        q, k, v, qkv, z, ba = _inproj(yp, *proj_w)
        oda = _attn_prompt(q, k, v, rel_bias, w["lamv"], w["subln"], bp_, seq, lambda_init)
        og, ssm_p = _gdn(qkv, z, ba, zero_conv, zero_ssm, *gdn_w, bp_, seq, seq)
        yp = _mixers_to_output(yp, oda, og, w)
        outs[0].append(k.reshape(bp_, seq, DA_HEADS, 2, DA_HD))
        outs[1].append(v.reshape(bp_, seq, DA_HEADS, DA_VD))
        outs[2].append(ssm_p)
        outs[3].append(qkv.reshape(bp_, seq, g_qkv)[:, seq - (CONV_W - 1):])

        q, k, v, qkv, z, ba = _inproj(ys, *proj_w)
        to3 = lambda a: a.reshape(bd, n_new, a.shape[-1])
        q4 = jnp.pad(q.reshape(bd, n_new, 2 * DA_HEADS, DA_HD), ((0, 0), (0, SUBLANES - n_new), (0, 0), (0, 0)))
        qblk = jnp.einsum("btgd,gf->bgtfd", q4, jnp.eye(2 * DA_HEADS, dtype=F32)).reshape(bd, -1, width)
        oda = _attn_sample(qblk, to3(k), to3(v), cache_k2, cache_v2, page_table, l * n_phys, rbt,
                           w["lamv"], w["subln"], lambda_init)[:, :n_new].reshape(bd * n_new, width)
        pad_tok = lambda a: jnp.pad(to3(a), ((0, 0), (0, pad_s - n_new), (0, 0))).reshape(bd * pad_s, a.shape[-1])
        conv0 = jnp.pad(state_conv[l], ((0, 0), (SUBLANES - (CONV_W - 1), 0), (0, 0)))
        og, ssm_s = _gdn(pad_tok(qkv), pad_tok(z), pad_tok(ba), conv0, state_ssm[l], *gdn_w, bd, pad_s, n_new)
        og = og.reshape(bd, pad_s, -1)[:, :n_new].reshape(bd * n_new, -1)
        ys = _mixers_to_output(ys, oda, og, w)
        conv_all = jnp.concatenate([state_conv[l], to3(qkv)], axis=1)
        outs[4].append(k.reshape(bd, n_new, DA_HEADS, 2, DA_HD))
        outs[5].append(v.reshape(bd, n_new, DA_HEADS, DA_VD))
        outs[6].append(ssm_s)
        outs[7].append(conv_all[:, -(CONV_W - 1):])

    return (yp.reshape(bp_, seq, d), ys.reshape(bd, n_new, d)) + tuple(jnp.stack(o) for o in outs)
```

```python
import functools
import math

import numpy as np
import jax
import jax.numpy as jnp
from jax import lax
from jax.experimental import pallas as pl
from jax.experimental.pallas import tpu as pltpu

F32 = jnp.float32
BF16 = jnp.bfloat16
HI = lax.Precision.HIGHEST

EPS = 1e-6
NEG = -1e30
PAGE_SIZE = 128
DA_HEADS = 4
DA_HD = 64
DA_VD = 2 * DA_HD
GDN_HEADS = 4
GDN_DK = 128
GDN_DV = 128
CONV_W = 4
GDN_CHUNK = 64
REL_BUCKETS = 32
REL_MAX_DIST = 128
PEER_HEADS = 8
N_KEYS = 128
PEER_TOPK = 16
LANES = 128
SUBLANES = 8
BF16_ROWS = 16

PROJ_TM = 256
ATTN_TQ = 512
PAGES_PER_STEP = 8
GDN_TB = 256
PEER_TB = 512
PEER_EC = 1024
PEER_TW = 256
VMEM_LIMIT = 56 * 1024 * 1024


def _dot(a, b, prec=None):
    return jnp.dot(a, b, preferred_element_type=F32, precision=prec)


def _dot_nt(a, b, prec=None):
    return lax.dot_general(a, b, (((1,), (1,)), ((), ())), preferred_element_type=F32, precision=prec)


def _split(a, terms=2):
    parts = []
    for _ in range(terms - 1):
        hi = a.astype(BF16)
        parts.append(hi)
        a = a - hi.astype(F32)
    parts.append(a.astype(BF16))
    return tuple(parts)


def _dot3(a_parts, b_parts):
    n = max(len(a_parts), len(b_parts))
    terms = [_dot(a, b) for i, a in enumerate(a_parts) for j, b in enumerate(b_parts) if i + j < n]
    return functools.reduce(lambda s, t: s + t, terms)


def _params(n_axes, vmem=VMEM_LIMIT):
    return pltpu.CompilerParams(dimension_semantics=("arbitrary",) * n_axes, vmem_limit_bytes=vmem)


def _bucket_lower_bounds():
    n = np.arange(REL_MAX_DIST + 1)
    max_exact = REL_BUCKETS // 2
    nf = np.maximum(n, 1).astype(np.float32)
    large = max_exact + (np.log(nf / max_exact) / math.log(REL_MAX_DIST / max_exact)
                         * (REL_BUCKETS - max_exact)).astype(np.int32)
    bucket = np.where(n < max_exact, n, np.minimum(large, REL_BUCKETS - 1))
    assert bucket[-1] == REL_BUCKETS - 1 and np.all(np.diff(bucket) >= 0)
    return [int(np.argmax(bucket >= b)) for b in range(REL_BUCKETS)]


_BUCKET_LO = _bucket_lower_bounds()


def _bias_from_dist(dist, bias_of_bucket):
    val = bias_of_bucket(REL_BUCKETS - 1)
    for b in range(REL_BUCKETS - 2, -1, -1):
        val = jnp.where(dist < _BUCKET_LO[b + 1], bias_of_bucket(b), val)
    return val


def _group_sum(x2, grp):
    hi = x2.astype(BF16)
    lo = (x2 - hi.astype(F32)).astype(BF16)
    return _dot(hi, grp) + _dot(lo, grp)


def _lambda(lam_ref, lambda_init):
    lv = lam_ref[...]
    t1 = jnp.sum(lv[0:1] * lv[1:2], axis=-1, keepdims=True)
    t2 = jnp.sum(lv[2:3] * lv[3:4], axis=-1, keepdims=True)
    return jnp.exp(t1) - jnp.exp(t2) + lambda_init


def _softmax_update(m, l, acc, s, v):
    m_new = jnp.maximum(m, jnp.max(s, axis=-1, keepdims=True))
    alpha = jnp.exp(m - m_new)
    p = jnp.exp(s - m_new)
    l_new = alpha * l + jnp.sum(p, axis=-1, keepdims=True)
    acc_new = alpha * acc + _dot(p.astype(BF16), v)
    return m_new, l_new, acc_new


def _inproj_kernel(x_ref, g1_ref, wqk_ref, wv_ref, wg_ref, wz_ref, wba_ref, gq_ref, gk_ref, grp_ref,
                   q_ref, k_ref, v_ref, qkv_ref, z_ref, ba_ref):
    x = x_ref[...]
    ms = jnp.mean(x * x, axis=-1, keepdims=True)
    xn = (x * lax.rsqrt(ms + EPS) * g1_ref[...]).astype(BF16)
    qk = _dot(xn, wqk_ref[...])
    half = qk.shape[1] // 2
    grp = grp_ref[...]

    def qk_norm(t, g):
        ms_g = _group_sum(t * t, grp) * (1.0 / DA_HD)
        return t * lax.rsqrt(ms_g + EPS) * g

    q_ref[...] = qk_norm(qk[:, :half], gq_ref[...]) * (DA_HD ** -0.5)
    k_ref[...] = qk_norm(qk[:, half:], gk_ref[...])
    v_ref[...] = _dot(xn, wv_ref[...])
    qkv_ref[...] = _dot(xn, wg_ref[...])
    z_ref[...] = _dot(xn, wz_ref[...])
    ba_ref[...] = _dot(xn, wba_ref[...])


def _inproj(x, g1, wqk, wv, wg, wz, wba, gq, gk, grp):
    t, d = x.shape
    tm = min(PROJ_TM, t)
    row = lambda i: (i, 0)
    fixed = lambda i: (0, 0)
    widths = (wqk.shape[1] // 2, wqk.shape[1] // 2, wv.shape[1], wg.shape[1], wz.shape[1], wba.shape[1])
    return pl.pallas_call(
        _inproj_kernel,
        grid=(t // tm,),
        in_specs=[pl.BlockSpec((tm, d), row), pl.BlockSpec(g1.shape, fixed)]
        + [pl.BlockSpec(w.shape, fixed) for w in (wqk, wv, wg, wz, wba, gq, gk, grp)],
        out_specs=[pl.BlockSpec((tm, w), row) for w in widths],
        out_shape=[jax.ShapeDtypeStruct((t, w), F32) for w in widths],
        compiler_params=_params(1),
    )(x, g1, wqk, wv, wg, wz, wba, gq, gk, grp)


def _attn_prompt_kernel(rb_ref, q_ref, k_ref, v_ref, lam_ref, sg_ref, o_ref, bias_scr, kb_scr, vb_scr,
                        *, tq, lambda_init):
    h = pl.program_id(0)
    b = pl.program_id(1)
    qi = pl.program_id(2)

    @pl.when((b == 0) & (qi == 0))
    def _():
        r = lax.broadcasted_iota(jnp.int32, (tq, tq), 0)
        c = lax.broadcasted_iota(jnp.int32, (tq, tq), 1)
        for which in range(2):
            dist = r - c + which * tq
            val = _bias_from_dist(dist, lambda bk: rb_ref[bk, h])
            bias_scr[which] = jnp.where(dist >= 0, val, NEG)

    @pl.when(qi == 0)
    def _():
        kb_scr[...] = k_ref[...].astype(BF16)
        vb_scr[...] = v_ref[...].astype(BF16)

    q = q_ref[...]
    lane = lax.broadcasted_iota(jnp.int32, q.shape, 1)
    qs = (jnp.where(lane < DA_HD, q, 0.0).astype(BF16), jnp.where(lane >= DA_HD, q, 0.0).astype(BF16))
    far_bias = rb_ref[REL_BUCKETS - 1, h]

    def block(j, bias, carry):
        start = pl.multiple_of(j * tq, tq)
        kblk = kb_scr[pl.ds(start, tq), :]
        vblk = vb_scr[pl.ds(start, tq), :]
        return tuple(_softmax_update(*carry[i], _dot_nt(qs[i], kblk) + bias, vblk) for i in range(2))

    init = (jnp.full((tq, 1), NEG, F32), jnp.zeros((tq, 1), F32), jnp.zeros((tq, DA_VD), F32))
    carry = block(qi, bias_scr[0], (init, init))
    carry = block(jnp.maximum(qi - 1, 0), bias_scr[1] + jnp.where(qi == 0, NEG, 0.0), carry)
    carry = lax.fori_loop(0, jnp.maximum(qi - 1, 0), lambda j, cr: block(j, far_bias, cr), carry)

    (_, l1, acc1), (_, l2, acc2) = carry
    o = acc1 * (1.0 / l1) - _lambda(lam_ref, lambda_init) * (acc2 * (1.0 / l2))
    o = o * lax.rsqrt(jnp.mean(o * o, axis=-1, keepdims=True) + EPS) * sg_ref[...] * (1.0 - lambda_init)
    o_ref[...] = o.astype(o_ref.dtype)


def _attn_prompt(q, k, v, rel_bias, lamv, subln_g, n_batch, seq, lambda_init):
    tq = min(ATTN_TQ, seq)
    nq = seq // tq
    kern = functools.partial(_attn_prompt_kernel, tq=tq, lambda_init=lambda_init)
    kv_spec = pl.BlockSpec((seq, DA_VD), lambda h, b, i: (b, h))
    q_spec = pl.BlockSpec((tq, DA_VD), lambda h, b, i: (b * nq + i, h))
    return pl.pallas_call(
        kern,
        grid=(DA_HEADS, n_batch, nq),
        in_specs=[pl.BlockSpec(memory_space=pltpu.SMEM), q_spec, kv_spec, kv_spec,
                  pl.BlockSpec(lamv.shape, lambda h, b, i: (0, 0)),
                  pl.BlockSpec(subln_g.shape, lambda h, b, i: (0, 0))],
        out_specs=q_spec,
        out_shape=jax.ShapeDtypeStruct(q.shape, BF16),
        scratch_shapes=[pltpu.VMEM((2, tq, tq), F32), pltpu.VMEM((seq, DA_VD), BF16),
                        pltpu.VMEM((seq, DA_VD), BF16)],
        compiler_params=_params(3),
    )(rel_bias, q, k, v, lamv, subln_g)


def _attn_sample_kernel(pt_ref, q_ref, kn_ref, vn_ref, rbt_ref, lam_ref, sg_ref, *rest,
                        pps, n_new, lambda_init):
    del pt_ref
    k_pages = rest[:pps]
    v_pages = rest[pps:2 * pps]
    o_ref, kcat, vcat, m_scr, l_scr, acc_scr = rest[2 * pps:]
    p = pl.program_id(1)
    last = pl.num_programs(1) - 1
    rows = 2 * DA_HEADS * SUBLANES
    ps = PAGE_SIZE

    @pl.when(p == 0)
    def _():
        m_scr[...] = jnp.full(m_scr.shape, NEG, F32)
        l_scr[...] = jnp.zeros(l_scr.shape, F32)
        acc_scr[...] = jnp.zeros(acc_scr.shape, F32)

    qblk = q_ref[...].astype(BF16)
    rbt = rbt_ref[...]
    far_bias = rbt[:, REL_BUCKETS - 1:REL_BUCKETS]

    for i in range(pps):
        kcat[i * ps:(i + 1) * ps, :] = k_pages[i][...].astype(BF16)
        vcat[i * ps:(i + 1) * ps, :] = v_pages[i][...].astype(BF16)
    s = _dot_nt(qblk, kcat[...])
    tok = lax.broadcasted_iota(jnp.int32, (rows, ps), 0) % SUBLANES
    off = lax.broadcasted_iota(jnp.int32, (rows, ps), 1)
    near = _bias_from_dist(ps + tok - off, lambda bk: rbt[:, bk:bk + 1])
    tail_bias = jnp.where(p == last, near, far_bias)
    split = (pps - 1) * ps
    s = jnp.concatenate([s[:, :split] + far_bias, s[:, split:] + tail_bias], axis=1)
    m, l, acc = _softmax_update(m_scr[...], l_scr[...], acc_scr[...], s, vcat[...])
    m_scr[...] = m
    l_scr[...] = l
    acc_scr[...] = acc

    @pl.when(p == last)
    def _():
        qf = qblk.astype(F32)
        kn = kn_ref[...].astype(BF16).astype(F32)
        vn = vn_ref[...].astype(BF16).astype(F32)
        tok_c = tok[:, 0:1]
        cols = []
        for tp in range(n_new):
            sc = jnp.sum(qf * kn[tp:tp + 1], axis=-1, keepdims=True)
            dist = tok_c - tp
            bias = rbt[:, n_new - 1:n_new]
            for dd in range(n_new - 2, -1, -1):
                bias = jnp.where(dist == dd, rbt[:, dd:dd + 1], bias)
            cols.append(jnp.where(dist >= 0, sc + bias, NEG))
        m_old = m_scr[...]
        m_new = _tree(jnp.maximum, [m_old] + cols)
        alpha = jnp.exp(m_old - m_new)
        l2 = alpha * l_scr[...]
        acc2 = alpha * acc_scr[...]
        for tp in range(n_new):
            pw = jnp.exp(cols[tp] - m_new)
            l2 = l2 + pw
            acc2 = acc2 + pw.astype(BF16).astype(F32) * vn[tp:tp + 1]
        a = acc2 * (1.0 / l2)
        lam = _lambda(lam_ref, lambda_init)
        outs = []
        for hh in range(DA_HEADS):
            r0 = hh * 2 * SUBLANES
            cols_h = slice(hh * DA_VD, (hh + 1) * DA_VD)
            o = a[r0:r0 + SUBLANES, cols_h] - lam * a[r0 + SUBLANES:r0 + 2 * SUBLANES, cols_h]
            o = o * lax.rsqrt(jnp.mean(o * o, axis=-1, keepdims=True) + EPS) * sg_ref[...] * (1.0 - lambda_init)
            outs.append(o)
        o_ref[...] = jnp.concatenate(outs, axis=1)


def _attn_sample(q3, k3, v3, cache_k2, cache_v2, page_table, layer_base, rbt, lamv, subln_g, lambda_init):
    n_dec, n_new, width = k3.shape
    n_pages = page_table.shape[1]
    pps = math.gcd(PAGES_PER_STEP, n_pages)
    assert n_new <= SUBLANES <= _BUCKET_LO[REL_BUCKETS // 2]
    rows = 2 * DA_HEADS * SUBLANES
    kern = functools.partial(_attn_sample_kernel, pps=pps, n_new=n_new, lambda_init=lambda_init)
    tok_spec = pl.BlockSpec((None, n_new, width), lambda b, p, pt: (b, 0, 0))
    q_spec = pl.BlockSpec((None, rows, width), lambda b, p, pt: (b, 0, 0))
    o_spec = pl.BlockSpec((None, SUBLANES, width), lambda b, p, pt: (b, 0, 0))
    fixed = lambda shape: pl.BlockSpec(shape, lambda b, p, pt: (0,) * len(shape))

    def page_spec(i):
        return pl.BlockSpec((None, PAGE_SIZE, width),
                            lambda b, p, pt: (layer_base + pt[b * n_pages + p * pps + i], 0, 0))

    grid_spec = pltpu.PrefetchScalarGridSpec(
        num_scalar_prefetch=1,
        grid=(n_dec, n_pages // pps),
        in_specs=[q_spec, tok_spec, tok_spec, fixed(rbt.shape), fixed(lamv.shape), fixed(subln_g.shape)]
        + [page_spec(i) for i in range(pps)] * 2,
        out_specs=o_spec,
        scratch_shapes=[pltpu.VMEM((pps * PAGE_SIZE, width), BF16), pltpu.VMEM((pps * PAGE_SIZE, width), BF16),
                        pltpu.VMEM((rows, 1), F32), pltpu.VMEM((rows, 1), F32), pltpu.VMEM((rows, width), F32)],
    )
    return pl.pallas_call(
        kern, grid_spec=grid_spec, out_shape=jax.ShapeDtypeStruct((n_dec, SUBLANES, width), F32),
        compiler_params=_params(2),
    )(page_table.reshape(-1), q3, k3, v3, rbt, lamv, subln_g, *([cache_k2] * pps), *([cache_v2] * pps))


def _gdn_kernel(qkv_ref, z_ref, ba_ref, conv0_ref, s0_ref, cw_ref, alog_ref, dtb_ref, gn_ref, tri_ref,
                o_ref, s_ref, ext_scr, *, tb, valid_len):
    blk = pl.program_id(1)
    hk = GDN_HEADS * GDN_DK
    c_len = GDN_CHUNK

    @pl.when(blk == 0)
    def _():
        ext_scr[0:SUBLANES, :] = conv0_ref[...]
        s_ref[...] = s0_ref[...]

    ext_scr[SUBLANES:SUBLANES + tb, :] = qkv_ref[...]
    cw = cw_ref[...]
    y = ext_scr[SUBLANES:SUBLANES + tb, :] * cw[CONV_W - 1:CONV_W]
    for j in range(1, CONV_W):
        y = y + ext_scr[SUBLANES - j:SUBLANES - j + tb, :] * cw[CONV_W - 1 - j:CONV_W - j]
    ext_scr[0:SUBLANES, :] = ext_scr[tb:tb + SUBLANES, :]
    y = y * jax.nn.sigmoid(y)

    ba = ba_ref[...]
    beta_all = jax.nn.sigmoid(ba)
    sp_in = ba + dtb_ref[...]
    softplus = jnp.maximum(sp_in, 0.0) + jnp.log(1.0 + jnp.exp(-jnp.abs(sp_in)))
    g_all = -jnp.exp(alog_ref[...]) * softplus
    if valid_len < tb:
        ok = lax.broadcasted_iota(jnp.int32, (tb, 1), 0) < valid_len
        y = jnp.where(ok, y, 0.0)
        beta_all = jnp.where(ok, beta_all, 0.0)
        g_all = jnp.where(ok, g_all, 0.0)

    ii = lax.broadcasted_iota(jnp.int32, (c_len, c_len), 0)
    jj = lax.broadcasted_iota(jnp.int32, (c_len, c_len), 1)
    eye = (ii == jj).astype(F32)
    z = z_ref[...]
    tri = tri_ref[...].astype(BF16)
    n_chunks = tb // c_len
    units = [(c, h) for c in range(n_chunks) for h in range(GDN_HEADS)]
    rows = lambda c: slice(c * c_len, (c + 1) * c_len)
    gcs = {}
    for c in range(n_chunks):
        gc_all = _dot3((tri,), _split(g_all[rows(c)], 3))
        gcs[c] = (gc_all, gc_all.T)
    st = {}
    for c, h in units:
        rs = rows(c)
        q = y[rs, h * GDN_DK:(h + 1) * GDN_DK]
        k = y[rs, hk + h * GDN_DK:hk + (h + 1) * GDN_DK]
        v = y[rs, 2 * hk + h * GDN_DV:2 * hk + (h + 1) * GDN_DV]
        q = q * lax.rsqrt(jnp.sum(q * q, axis=-1, keepdims=True) + EPS) * (GDN_DK ** -0.5)
        k = k * lax.rsqrt(jnp.sum(k * k, axis=-1, keepdims=True) + EPS)
        beta = beta_all[rs, h:h + 1]
        gc = gcs[c][0][:, GDN_HEADS + h:GDN_HEADS + h + 1]
        gr = gcs[c][1][GDN_HEADS + h:GDN_HEADS + h + 1, :]
        decay = jnp.where(ii >= jj, jnp.exp(jnp.minimum(gc - gr, 0.0)), 0.0)
        kb = k * beta
        k16 = k.astype(BF16)
        egc = jnp.exp(gc)
        g_last = gc[c_len - 1:c_len, :]
        st[c, h] = dict(
            x=-jnp.where(ii > jj, _dot_nt(kb.astype(BF16), k16) * decay, 0.0),
            rhs=_split(jnp.concatenate([v * beta, kb * egc], axis=1)),
            qk=(_dot_nt(q.astype(BF16), k16) * decay).astype(BF16),
            qe=(q * egc).astype(BF16),
            kd_t=(k * jnp.exp(g_last - gc)).T.astype(BF16),
            e_last=jnp.exp(g_last))
    for u_ in units:
        st[u_]["t"] = eye + st[u_]["x"]
        st[u_]["xs"] = _split(st[u_]["x"])
    for _ in range(int(math.log2(c_len)) - 1):
        for u_ in units:
            st[u_]["xs"] = _split(_dot3(st[u_]["xs"], st[u_]["xs"]))
        for u_ in units:
            st[u_]["t"] = st[u_]["t"] + _dot3(_split(st[u_]["t"]), st[u_]["xs"])
    for u_ in units:
        sol = _dot3(_split(st[u_]["t"]), st[u_]["rhs"])
        st[u_]["u"] = sol[:, :GDN_DV]
        st[u_]["w"] = sol[:, GDN_DV:].astype(BF16)
    state = [s_ref[h] for h in range(GDN_HEADS)]
    for c in range(n_chunks):
        for h in range(GDN_HEADS):
            cs = slice(h * GDN_DK, (h + 1) * GDN_DK)
            d = st[c, h]
            s16 = state[h].astype(BF16)
            v_new = d["u"] - _dot(d["w"], s16)
            o = _dot(d["qe"], s16) + _dot(d["qk"], v_new.astype(BF16))
            state[h] = state[h] * d["e_last"] + _dot(d["kd_t"], v_new.astype(BF16))
            o = o * lax.rsqrt(jnp.mean(o * o, axis=-1, keepdims=True) + EPS) * gn_ref[...]
            zz = z[rows(c), cs]
            o_ref[rows(c), cs] = (o * (zz * jax.nn.sigmoid(zz))).astype(o_ref.dtype)
    for h in range(GDN_HEADS):
        s_ref[h] = state[h]


def _gdn(qkv, z, ba, conv0, s0, cw, alog, dtb, gn, tri, n_batch, seq_pad, valid_len):
    tb = min(GDN_TB, seq_pad)
    nblk = seq_pad // tb
    assert valid_len == seq_pad or nblk == 1
    kern = functools.partial(_gdn_kernel, tb=tb, valid_len=min(valid_len, tb))
    row = lambda w: pl.BlockSpec((tb, w), lambda b, i: (b * nblk + i, 0))
    fixed = lambda a: pl.BlockSpec(a.shape, lambda b, i: (0,) * a.ndim)
    state_spec = pl.BlockSpec((None,) + s0.shape[1:], lambda b, i: (b, 0, 0, 0))
    return pl.pallas_call(
        kern,
        grid=(n_batch, nblk),
        in_specs=[row(qkv.shape[1]), row(z.shape[1]), row(ba.shape[1]),
                  pl.BlockSpec((None,) + conv0.shape[1:], lambda b, i: (b, 0, 0)), state_spec,
                  fixed(cw), fixed(alog), fixed(dtb), fixed(gn), fixed(tri)],
        out_specs=[row(z.shape[1]), state_spec],
        out_shape=[jax.ShapeDtypeStruct(z.shape, BF16), jax.ShapeDtypeStruct(s0.shape, F32)],
        scratch_shapes=[pltpu.VMEM((tb + SUBLANES, qkv.shape[1]), F32)],
        compiler_params=_params(2),
    )(qkv, z, ba, conv0, s0, cw, alog, dtb, gn, tri)


def _outproj_kernel(x_ref, oda_ref, og_ref, wo1_ref, wo2_ref, g2_ref, wq_ref, sk_ref, h_ref, hnt_ref, st_ref):
    hres = x_ref[...] + _dot(oda_ref[...].astype(BF16), wo1_ref[...]) + _dot(og_ref[...], wo2_ref[...])
    h_ref[...] = hres
    ms = jnp.mean(hres * hres, axis=-1, keepdims=True)
    hn = hres * lax.rsqrt(ms + EPS) * g2_ref[...]
    hnt_ref[...] = hn.T.astype(BF16)
    q = _dot(hn.astype(BF16), wq_ref[...]).astype(BF16)
    for hs in range(sk_ref.shape[0]):
        st_ref[hs] = _dot_nt(sk_ref[hs], q[:, hs * LANES:(hs + 1) * LANES])


def _outproj(x, oda, og, wo1, wo2, g2, wq, sk):
    t, d = x.shape
    tm = min(PROJ_TM, t)
    row = lambda w: pl.BlockSpec((tm, w), lambda i: (i, 0))
    fixed = lambda a: pl.BlockSpec(a.shape, lambda i: (0,) * a.ndim)
    return pl.pallas_call(
        _outproj_kernel,
        grid=(t // tm,),
        in_specs=[row(d), row(oda.shape[1]), row(og.shape[1])] + [fixed(a) for a in (wo1, wo2, g2, wq, sk)],
        out_specs=[row(d), pl.BlockSpec((d, tm), lambda i: (0, i)),
                   pl.BlockSpec((sk.shape[0], N_KEYS, tm), lambda i: (0, 0, i))],
        out_shape=[jax.ShapeDtypeStruct((t, d), F32), jax.ShapeDtypeStruct((d, t), BF16),
                   jax.ShapeDtypeStruct((sk.shape[0], N_KEYS, t), F32)],
        compiler_params=_params(1),
    )(x, oda, og, wo1, wo2, g2, wq, sk)


def _bitonic_merge_desc(lst):
    n = len(lst)
    j = n // 2
    while j >= 1:
        for i in range(n):
            l = i ^ j
            if l > i:
                lst[i], lst[l] = jnp.maximum(lst[i], lst[l]), jnp.minimum(lst[i], lst[l])
        j //= 2
    return lst


def _bitonic_sort_desc(lst):
    n = len(lst)
    k = 2
    while k <= n:
        j = k // 2
        while j >= 1:
            for i in range(n):
                l = i ^ j
                if l > i:
                    hi, lo = jnp.maximum(lst[i], lst[l]), jnp.minimum(lst[i], lst[l])
                    lst[i], lst[l] = (hi, lo) if (i & k) == 0 else (lo, hi)
            j //= 2
        k *= 2
    return lst


def _tree(fn, xs):
    xs = list(xs)
    while len(xs) > 1:
        xs = [fn(xs[i], xs[i + 1]) if i + 1 < len(xs) else xs[i] for i in range(0, len(xs), 2)]
    return xs[0]


def _peer_topk_kernel(st_ref, ap_ref, cnt_ref, bp_ref, rank_ref):
    kk = PEER_TOPK
    n_vr = N_KEYS // SUBLANES
    tops = []
    for hs in range(2 * PEER_HEADS):
        x = st_ref[hs]
        lst = _bitonic_sort_desc([x[j * SUBLANES:(j + 1) * SUBLANES, :] for j in range(n_vr)])[:kk]
        shift = SUBLANES // 2
        while shift >= 1:
            other = [pltpu.roll(a, shift, 0) for a in lst]
            lst = _bitonic_merge_desc([jnp.maximum(lst[i], other[kk - 1 - i]) for i in range(kk)])
            shift //= 2
        tops.append(lst)

    sub = lax.broadcasted_iota(jnp.int32, tops[0][0].shape, 0)

    def pack(which, a):
        out = tops[which][a]
        for h in range(1, PEER_HEADS):
            out = jnp.where(sub == h, tops[2 * h + which][a], out)
        return out

    v1 = [pack(0, a) for a in range(kk)]
    v2 = [pack(1, a) for a in range(kk)]
    pairs = [(a, b) for a in range(kk) for b in range(kk) if (a + 1) * (b + 1) <= kk]
    cands = [v1[a] + v2[b] for a, b in pairs]
    cur = list(cands)
    for r in range(kk):
        tau = _tree(jnp.maximum, cur)
        if r + 1 < kk:
            cur = [jnp.where(c == tau, -jnp.inf, c) for c in cur]
    top = v1[0] + v2[0]
    zsum = _tree(jnp.add, [jnp.where(c >= tau, jnp.exp(c - top), 0.0) for c in cands])
    rz = 1.0 / zsum
    cnt_of_rank = [_tree(jnp.add, [jnp.where(c >= tau, 1.0, 0.0) for c, (a2, _) in zip(cands, pairs) if a2 == a])
                   for a in range(kk)]
    for h in range(PEER_HEADS):
        row = lambda a, h=h: a[h:h + 1, :]
        s1 = st_ref[2 * h]
        s2 = st_ref[2 * h + 1]
        ap_ref[h] = (jnp.exp(s1 - row(v1[0])) * row(rz)).astype(BF16).astype(F32)
        bp_ref[h] = pltpu.bitcast(jnp.exp(s2 - row(v2[0])).astype(BF16), jnp.uint32)
        cnt = jnp.zeros(s1.shape, F32)
        rank = jnp.full(s2.shape, float(kk), F32)
        for r in range(kk - 1, -1, -1):
            cnt = jnp.where(s1 == row(v1[r]), row(cnt_of_rank[r]), cnt)
            rank = jnp.where(s2 == row(v2[r]), float(r), rank)
        cnt_ref[h] = cnt
        rank_ref[h] = pltpu.bitcast(rank.astype(BF16), jnp.uint32)


def _peer_topk(st):
    n_hs, n_keys, t = st.shape
    blk = pl.BlockSpec((PEER_HEADS, n_keys, LANES), lambda i: (0, 0, i))
    words = n_keys * 2 // 4
    wblk = pl.BlockSpec((PEER_HEADS, words, LANES), lambda i: (0, 0, i))
    f32_out = jax.ShapeDtypeStruct((PEER_HEADS, n_keys, t), F32)
    word_out = jax.ShapeDtypeStruct((PEER_HEADS, words, t), jnp.uint32)
    return pl.pallas_call(
        _peer_topk_kernel,
        grid=(t // LANES,),
        in_specs=[pl.BlockSpec((n_hs, n_keys, LANES), lambda i: (0, 0, i))],
        out_specs=[blk, blk, wblk, wblk],
        out_shape=[f32_out, f32_out, word_out, word_out],
        compiler_params=_params(1),
    )(st)


def _packed_row(row):
    bits = pltpu.bitcast(jnp.broadcast_to(row, (SUBLANES, LANES)), jnp.uint32)
    hi = bits >> 16
    return pltpu.bitcast(hi | (hi << 16), BF16)


def _gelu_tanh(x):
    return 0.5 * x * (1.0 + jnp.tanh(math.sqrt(2.0 / math.pi) * (x + 0.044715 * (x * x * x))))


def _peer_dense_kernel(hnt_ref, ap_ref, cnt_ref, bp_ref, rank_ref, u_ref, vt_ref, h_ref, y_ref, acc_scr, hid_scr,
                       w_scr, *, tb, ec):
    e = pl.program_id(1)
    rows_per_step = ec // N_KEYS
    n_pieces = N_KEYS // BF16_ROWS

    @pl.when(e == 0)
    def _():
        acc_scr[...] = jnp.zeros(acc_scr.shape, F32)

    tw = min(PEER_TW, tb)
    subs = [slice(t0, t0 + tw) for t0 in range(0, tb, tw)]
    for sb in subs:
        hid_scr[:, sb] = _dot(u_ref[...], hnt_ref[:, sb])
    for sb in subs:
        for tl in range(sb.start // LANES, sb.stop // LANES):
            ts = slice(tl * LANES, (tl + 1) * LANES)
            for i in range(rows_per_step):
                gates = [jnp.zeros((BF16_ROWS, LANES), BF16)] * n_pieces
                for h in range(PEER_HEADS):
                    a_row = _packed_row(ap_ref[h, i:i + 1, ts])
                    c_row = _packed_row(cnt_ref[h, i:i + 1, ts])
                    for j in range(n_pieces):
                        ws = slice(j * SUBLANES, (j + 1) * SUBLANES)
                        prod = a_row * pltpu.bitcast(bp_ref[h, ws, ts], BF16)
                        keep = pltpu.bitcast(rank_ref[h, ws, ts], BF16) < c_row
                        gates[j] = gates[j] + jnp.where(keep, prod, jnp.zeros_like(prod))
                for j in range(n_pieces):
                    rs = slice(i * N_KEYS + j * BF16_ROWS, i * N_KEYS + (j + 1) * BF16_ROWS)
                    w_scr[rs, ts] = gates[j] * _gelu_tanh(hid_scr[rs, ts].astype(BF16))
        acc_scr[:, sb] += _dot(vt_ref[...], w_scr[:, sb])

    @pl.when(e == pl.num_programs(1) - 1)
    def _():
        y_ref[...] = h_ref[...] + acc_scr[...].T


def _peer_dense(hnt, ap, cnt, bp, rank, u, vt, hres):
    d, t = hnt.shape
    n_exp = u.shape[0]
    tb = min(PEER_TB, t)
    ec = PEER_EC
    kern = functools.partial(_peer_dense_kernel, tb=tb, ec=ec)
    tok3 = pl.BlockSpec((PEER_HEADS, bp.shape[1], tb), lambda i, e: (0, 0, i))
    rows3 = pl.BlockSpec((PEER_HEADS, ec // N_KEYS, tb), lambda i, e: (0, e, i))
    return pl.pallas_call(
        kern,
        grid=(t // tb, n_exp // ec),
        in_specs=[pl.BlockSpec((d, tb), lambda i, e: (0, i)), rows3, rows3, tok3, tok3,
                  pl.BlockSpec((ec, d), lambda i, e: (e, 0)), pl.BlockSpec((d, ec), lambda i, e: (0, e)),
                  pl.BlockSpec((tb, d), lambda i, e: (i, 0))],
        out_specs=pl.BlockSpec((tb, d), lambda i, e: (i, 0)),
        out_shape=jax.ShapeDtypeStruct((t, d), F32),
        scratch_shapes=[pltpu.VMEM((d, tb), F32), pltpu.VMEM((ec, tb), F32), pltpu.VMEM((ec, tb), BF16)],
        compiler_params=_params(2),
    )(hnt, ap, cnt, bp, rank, u, vt, hres)


def _group_matrix(width, group):
    idx = np.arange(width) // group
    return jnp.asarray(idx[:, None] == idx[None, :], dtype=BF16)


def _pad_lanes(vec, offset, width=LANES):
    return jnp.zeros((1, width), F32).at[0, offset:offset + vec.shape[0]].set(vec)


def _layer_weights(l, p):
    d = p["w_in"].shape[1]
    da_qk = DA_HEADS * 2 * DA_HD
    da_v = DA_HEADS * DA_VD
    g_qkv = GDN_HEADS * (2 * GDN_DK + GDN_DV)
    g_z = GDN_HEADS * GDN_DV
    cuts = np.cumsum([0, da_qk, da_qk, da_v, g_qkv, g_z, GDN_HEADS, GDN_HEADS]).tolist()
    w_in = p["w_in"][l].astype(BF16)
    wba = jnp.zeros((d, LANES), BF16).at[:, :2 * GDN_HEADS].set(w_in[:, cuts[5]:cuts[7]])
    w_out = p["w_out"][l].astype(BF16)
    return dict(
        g1=p["norm1_g"][l][None, :], wqk=w_in[:, cuts[0]:cuts[2]], wv=w_in[:, cuts[2]:cuts[3]],
        wg=w_in[:, cuts[3]:cuts[4]], wz=w_in[:, cuts[4]:cuts[5]], wba=wba,
        gq=jnp.tile(p["da_q_g"][l], 2 * DA_HEADS)[None, :], gk=jnp.tile(p["da_k_g"][l], 2 * DA_HEADS)[None, :],
        grp=_group_matrix(da_qk, DA_HD),
        lamv=jnp.stack([p["lambda_q1"][l], p["lambda_k1"][l], p["lambda_q2"][l], p["lambda_k2"][l]]),
        subln=p["da_subln_g"][l][None, :],
        cw=jnp.zeros((SUBLANES, g_qkv), F32).at[:CONV_W].set(p["gdn_conv_w"][l]),
        alog=_pad_lanes(p["gdn_a_log"][l], GDN_HEADS), dtb=_pad_lanes(p["gdn_dt_bias"][l], GDN_HEADS),
        gn=p["gdn_norm_g"][l][None, :],
        tri=jnp.asarray(np.tril(np.ones((GDN_CHUNK, GDN_CHUNK), np.float32))),
        wo1=w_out[:da_v], wo2=w_out[da_v:], g2=p["norm2_g"][l][None, :],
        wq=p["peer_w_q"][l].astype(BF16),
        sk=p["peer_sub_keys"][l].reshape(2 * PEER_HEADS, N_KEYS, -1).astype(BF16),
        u=p["peer_u"][l].astype(BF16), vt=p["peer_v"][l].T.astype(BF16),
    )


def _mixers_to_output(x2, oda, og, w):
    hres, hnt, st = _outproj(x2, oda, og, w["wo1"], w["wo2"], w["g2"], w["wq"], w["sk"])
    ap, cnt, bp, rank = _peer_topk(st)
    return _peer_dense(hnt, ap, cnt, bp, rank, w["u"], w["vt"], hres)


def kernel(x_prompt, x_sample, cache_k, cache_v, state_ssm, state_conv, page_table, rel_bias, norm1_g, w_in,
           da_q_g, da_k_g, lambda_q1, lambda_k1, lambda_q2, lambda_k2, da_subln_g, gdn_conv_w, gdn_a_log,
           gdn_dt_bias, gdn_norm_g, w_out, norm2_g, peer_w_q, peer_sub_keys, peer_u, peer_v):
    p = dict(norm1_g=norm1_g, w_in=w_in, da_q_g=da_q_g, da_k_g=da_k_g, lambda_q1=lambda_q1, lambda_k1=lambda_k1,
             lambda_q2=lambda_q2, lambda_k2=lambda_k2, da_subln_g=da_subln_g, gdn_conv_w=gdn_conv_w,
             gdn_a_log=gdn_a_log, gdn_dt_bias=gdn_dt_bias, gdn_norm_g=gdn_norm_g, w_out=w_out, norm2_g=norm2_g,
             peer_w_q=peer_w_q, peer_sub_keys=peer_sub_keys, peer_u=peer_u, peer_v=peer_v)
    depth = w_in.shape[0]
    bp_, seq, d = x_prompt.shape
    bd, n_new, _ = x_sample.shape
    n_phys = cache_k.shape[1]
    width = DA_HEADS * DA_VD
    g_qkv = GDN_HEADS * (2 * GDN_DK + GDN_DV)
    cache_k2 = cache_k.reshape(depth * n_phys, PAGE_SIZE, width)
    cache_v2 = cache_v.reshape(depth * n_phys, PAGE_SIZE, width)
    rbt = jnp.repeat(rel_bias.T, 2 * SUBLANES, axis=0)
    pad_s = -(-n_new // GDN_CHUNK) * GDN_CHUNK
    zero_conv = jnp.zeros((bp_, SUBLANES, g_qkv), F32)
    zero_ssm = jnp.zeros((bp_,) + state_ssm.shape[2:], F32)

    yp = x_prompt.reshape(bp_ * seq, d)
    ys = x_sample.reshape(bd * n_new, d)
    outs = [[] for _ in range(8)]
    for l in range(depth):
        lambda_init = 0.8 - 0.6 * math.exp(-0.3 * l)
        w = _layer_weights(l, p)
        proj_w = (w["g1"], w["wqk"], w["wv"], w["wg"], w["wz"], w["wba"], w["gq"], w["gk"], w["grp"])
        gdn_w = (w["cw"], w["alog"], w["dtb"], w["gn"], w["tri"])

        q, k, v, qkv, z, ba = _inproj(yp, *proj_w)
        oda = _attn_prompt(q, k, v, rel_bias, w["lamv"], w["subln"], bp_, seq, lambda_init)
        og, ssm_p = _gdn(qkv, z, ba, zero_conv, zero_ssm, *gdn_w, bp_, seq, seq)
        yp = _mixers_to_output(yp, oda, og, w)
        outs[0].append(k.reshape(bp_, seq, DA_HEADS, 2, DA_HD))
        outs[1].append(v.reshape(bp_, seq, DA_HEADS, DA_VD))
        outs[2].append(ssm_p)
        outs[3].append(qkv.reshape(bp_, seq, g_qkv)[:, seq - (CONV_W - 1):])

        q, k, v, qkv, z, ba = _inproj(ys, *proj_w)
        to3 = lambda a: a.reshape(bd, n_new, a.shape[-1])
        q4 = jnp.pad(q.reshape(bd, n_new, 2 * DA_HEADS, DA_HD), ((0, 0), (0, SUBLANES - n_new), (0, 0), (0, 0)))
        qblk = jnp.einsum("btgd,gf->bgtfd", q4, jnp.eye(2 * DA_HEADS, dtype=F32)).reshape(bd, -1, width)
        oda = _attn_sample(qblk, to3(k), to3(v), cache_k2, cache_v2, page_table, l * n_phys, rbt,
                           w["lamv"], w["subln"], lambda_init)[:, :n_new].reshape(bd * n_new, width)
        pad_tok = lambda a: jnp.pad(to3(a), ((0, 0), (0, pad_s - n_new), (0, 0))).reshape(bd * pad_s, a.shape[-1])
        conv0 = jnp.pad(state_conv[l], ((0, 0), (SUBLANES - (CONV_W - 1), 0), (0, 0)))
        og, ssm_s = _gdn(pad_tok(qkv), pad_tok(z), pad_tok(ba), conv0, state_ssm[l], *gdn_w, bd, pad_s, n_new)
        og = og.reshape(bd, pad_s, -1)[:, :n_new].reshape(bd * n_new, -1)
        ys = _mixers_to_output(ys, oda, og, w)
        conv_all = jnp.concatenate([state_conv[l], to3(qkv)], axis=1)
        outs[4].append(k.reshape(bd, n_new, DA_HEADS, 2, DA_HD))
        outs[5].append(v.reshape(bd, n_new, DA_HEADS, DA_VD))
        outs[6].append(ssm_s)
        outs[7].append(conv_all[:, -(CONV_W - 1):])

    return (yp.reshape(bp_, seq, d), ys.reshape(bd, n_new, d)) + tuple(jnp.stack(o) for o in outs)
```

```python
import functools
import math

import numpy as np
import jax
import jax.numpy as jnp
from jax import lax
from jax.experimental import pallas as pl
from jax.experimental.pallas import tpu as pltpu

F32 = jnp.float32
BF16 = jnp.bfloat16
HI = lax.Precision.HIGHEST

EPS = 1e-6
NEG = -1e30
PAGE_SIZE = 128
DA_HEADS = 4
DA_HD = 64
DA_VD = 2 * DA_HD
GDN_HEADS = 4
GDN_DK = 128
GDN_DV = 128
CONV_W = 4
GDN_CHUNK = 64
REL_BUCKETS = 32
REL_MAX_DIST = 128
PEER_HEADS = 8
N_KEYS = 128
PEER_TOPK = 16
LANES = 128
SUBLANES = 8
BF16_ROWS = 16

PROJ_TM = 256
ATTN_TQ = 512
PAGES_PER_STEP = 8
GDN_TB = 256
PEER_TB = 512
PEER_EC = 1024
VMEM_LIMIT = 56 * 1024 * 1024


def _dot(a, b, prec=None):
    return jnp.dot(a, b, preferred_element_type=F32, precision=prec)


def _dot_nt(a, b, prec=None):
    return lax.dot_general(a, b, (((1,), (1,)), ((), ())), preferred_element_type=F32, precision=prec)


def _split(a, terms=2):
    parts = []
    for _ in range(terms - 1):
        hi = a.astype(BF16)
        parts.append(hi)
        a = a - hi.astype(F32)
    parts.append(a.astype(BF16))
    return tuple(parts)


def _dot3(a_parts, b_parts):
    n = max(len(a_parts), len(b_parts))
    terms = [_dot(a, b) for i, a in enumerate(a_parts) for j, b in enumerate(b_parts) if i + j < n]
    return functools.reduce(lambda s, t: s + t, terms)


def _params(n_axes, vmem=VMEM_LIMIT):
    return pltpu.CompilerParams(dimension_semantics=("arbitrary",) * n_axes, vmem_limit_bytes=vmem)


def _bucket_lower_bounds():
    n = np.arange(REL_MAX_DIST + 1)
    max_exact = REL_BUCKETS // 2
    nf = np.maximum(n, 1).astype(np.float32)
    large = max_exact + (np.log(nf / max_exact) / math.log(REL_MAX_DIST / max_exact)
                         * (REL_BUCKETS - max_exact)).astype(np.int32)
    bucket = np.where(n < max_exact, n, np.minimum(large, REL_BUCKETS - 1))
    assert bucket[-1] == REL_BUCKETS - 1 and np.all(np.diff(bucket) >= 0)
    return [int(np.argmax(bucket >= b)) for b in range(REL_BUCKETS)]


_BUCKET_LO = _bucket_lower_bounds()


def _bias_from_dist(dist, bias_of_bucket):
    val = bias_of_bucket(REL_BUCKETS - 1)
    for b in range(REL_BUCKETS - 2, -1, -1):
        val = jnp.where(dist < _BUCKET_LO[b + 1], bias_of_bucket(b), val)
    return val


def _group_sum(x2, grp):
    hi = x2.astype(BF16)
    lo = (x2 - hi.astype(F32)).astype(BF16)
    return _dot(hi, grp) + _dot(lo, grp)


def _lambda(lam_ref, lambda_init):
    lv = lam_ref[...]
    t1 = jnp.sum(lv[0:1] * lv[1:2], axis=-1, keepdims=True)
    t2 = jnp.sum(lv[2:3] * lv[3:4], axis=-1, keepdims=True)
    return jnp.exp(t1) - jnp.exp(t2) + lambda_init


def _softmax_update(m, l, acc, s, v):
    m_new = jnp.maximum(m, jnp.max(s, axis=-1, keepdims=True))
    alpha = jnp.exp(m - m_new)
    p = jnp.exp(s - m_new)
    l_new = alpha * l + jnp.sum(p, axis=-1, keepdims=True)
    acc_new = alpha * acc + _dot(p.astype(BF16), v)
    return m_new, l_new, acc_new


def _inproj_kernel(x_ref, g1_ref, wqk_ref, wv_ref, wg_ref, wz_ref, wba_ref, gq_ref, gk_ref, grp_ref,
                   q_ref, k_ref, v_ref, qkv_ref, z_ref, ba_ref):
    x = x_ref[...]
    ms = jnp.mean(x * x, axis=-1, keepdims=True)
    xn = (x * lax.rsqrt(ms + EPS) * g1_ref[...]).astype(BF16)
    qk = _dot(xn, wqk_ref[...])
    half = qk.shape[1] // 2
    grp = grp_ref[...]

    def qk_norm(t, g):
        ms_g = _group_sum(t * t, grp) * (1.0 / DA_HD)
        return t * lax.rsqrt(ms_g + EPS) * g

    q_ref[...] = qk_norm(qk[:, :half], gq_ref[...]) * (DA_HD ** -0.5)
    k_ref[...] = qk_norm(qk[:, half:], gk_ref[...])
    v_ref[...] = _dot(xn, wv_ref[...])
    qkv_ref[...] = _dot(xn, wg_ref[...])
    z_ref[...] = _dot(xn, wz_ref[...])
    ba_ref[...] = _dot(xn, wba_ref[...])


def _inproj(x, g1, wqk, wv, wg, wz, wba, gq, gk, grp):
    t, d = x.shape
    tm = min(PROJ_TM, t)
    row = lambda i: (i, 0)
    fixed = lambda i: (0, 0)
    widths = (wqk.shape[1] // 2, wqk.shape[1] // 2, wv.shape[1], wg.shape[1], wz.shape[1], wba.shape[1])
    return pl.pallas_call(
        _inproj_kernel,
        grid=(t // tm,),
        in_specs=[pl.BlockSpec((tm, d), row), pl.BlockSpec(g1.shape, fixed)]
        + [pl.BlockSpec(w.shape, fixed) for w in (wqk, wv, wg, wz, wba, gq, gk, grp)],
        out_specs=[pl.BlockSpec((tm, w), row) for w in widths],
        out_shape=[jax.ShapeDtypeStruct((t, w), F32) for w in widths],
        compiler_params=_params(1),
    )(x, g1, wqk, wv, wg, wz, wba, gq, gk, grp)


def _attn_prompt_kernel(rb_ref, q_ref, k_ref, v_ref, lam_ref, sg_ref, o_ref, bias_scr, kb_scr, vb_scr,
                        *, tq, lambda_init):
    h = pl.program_id(0)
    b = pl.program_id(1)
    qi = pl.program_id(2)

    @pl.when((b == 0) & (qi == 0))
    def _():
        r = lax.broadcasted_iota(jnp.int32, (tq, tq), 0)
        c = lax.broadcasted_iota(jnp.int32, (tq, tq), 1)
        for which in range(2):
            dist = r - c + which * tq
            val = _bias_from_dist(dist, lambda bk: rb_ref[bk, h])
            bias_scr[which] = jnp.where(dist >= 0, val, NEG)

    @pl.when(qi == 0)
    def _():
        kb_scr[...] = k_ref[...].astype(BF16)
        vb_scr[...] = v_ref[...].astype(BF16)

    q = q_ref[...]
    lane = lax.broadcasted_iota(jnp.int32, q.shape, 1)
    qs = (jnp.where(lane < DA_HD, q, 0.0).astype(BF16), jnp.where(lane >= DA_HD, q, 0.0).astype(BF16))
    far_bias = rb_ref[REL_BUCKETS - 1, h]

    def block(j, bias, carry):
        start = pl.multiple_of(j * tq, tq)
        kblk = kb_scr[pl.ds(start, tq), :]
        vblk = vb_scr[pl.ds(start, tq), :]
        return tuple(_softmax_update(*carry[i], _dot_nt(qs[i], kblk) + bias, vblk) for i in range(2))

    init = (jnp.full((tq, 1), NEG, F32), jnp.zeros((tq, 1), F32), jnp.zeros((tq, DA_VD), F32))
    carry = block(qi, bias_scr[0], (init, init))
    carry = block(jnp.maximum(qi - 1, 0), bias_scr[1] + jnp.where(qi == 0, NEG, 0.0), carry)
    carry = lax.fori_loop(0, jnp.maximum(qi - 1, 0), lambda j, cr: block(j, far_bias, cr), carry)

    (_, l1, acc1), (_, l2, acc2) = carry
    o = acc1 * (1.0 / l1) - _lambda(lam_ref, lambda_init) * (acc2 * (1.0 / l2))
    o = o * lax.rsqrt(jnp.mean(o * o, axis=-1, keepdims=True) + EPS) * sg_ref[...] * (1.0 - lambda_init)
    o_ref[...] = o.astype(o_ref.dtype)


def _attn_prompt(q, k, v, rel_bias, lamv, subln_g, n_batch, seq, lambda_init):
    tq = min(ATTN_TQ, seq)
    nq = seq // tq
    kern = functools.partial(_attn_prompt_kernel, tq=tq, lambda_init=lambda_init)
    kv_spec = pl.BlockSpec((seq, DA_VD), lambda h, b, i: (b, h))
    q_spec = pl.BlockSpec((tq, DA_VD), lambda h, b, i: (b * nq + i, h))
    return pl.pallas_call(
        kern,
        grid=(DA_HEADS, n_batch, nq),
        in_specs=[pl.BlockSpec(memory_space=pltpu.SMEM), q_spec, kv_spec, kv_spec,
                  pl.BlockSpec(lamv.shape, lambda h, b, i: (0, 0)),
                  pl.BlockSpec(subln_g.shape, lambda h, b, i: (0, 0))],
        out_specs=q_spec,
        out_shape=jax.ShapeDtypeStruct(q.shape, BF16),
        scratch_shapes=[pltpu.VMEM((2, tq, tq), F32), pltpu.VMEM((seq, DA_VD), BF16),
                        pltpu.VMEM((seq, DA_VD), BF16)],
        compiler_params=_params(3),
    )(rel_bias, q, k, v, lamv, subln_g)


def _attn_sample_kernel(pt_ref, q_ref, kn_ref, vn_ref, rbt_ref, lam_ref, sg_ref, exp_ref, hmask_ref, *rest,
                        pps, n_new, lambda_init):
    del pt_ref
    k_pages = rest[:pps]
    v_pages = rest[pps:2 * pps]
    o_ref, kcat, vcat, m_scr, l_scr, acc_scr = rest[2 * pps:]
    vrows = DA_HEADS * PAGE_SIZE
    p = pl.program_id(1)
    last = pl.num_programs(1) - 1
    rows = 2 * DA_HEADS * SUBLANES
    ps = PAGE_SIZE

    @pl.when(p == 0)
    def _():
        m_scr[...] = jnp.full(m_scr.shape, NEG, F32)
        l_scr[...] = jnp.zeros(l_scr.shape, F32)
        acc_scr[...] = jnp.zeros(acc_scr.shape, F32)

    qblk = q_ref[...].astype(BF16)
    rbt = rbt_ref[...]
    far_bias = rbt[:, REL_BUCKETS - 1:REL_BUCKETS]

    for i in range(pps):
        vcat[i * vrows:(i + 1) * vrows, :] = v_pages[i][...].astype(BF16)
        for hh in range(DA_HEADS):
            for sm in range(2):
                c0 = hh * DA_VD + sm * DA_HD
                kcat[c0:c0 + DA_HD, i * ps:(i + 1) * ps] = k_pages[i][hh, sm].astype(BF16)
    s = _dot(qblk, kcat[...])
    tok = lax.broadcasted_iota(jnp.int32, (rows, ps), 0) % SUBLANES
    off = lax.broadcasted_iota(jnp.int32, (rows, ps), 1)
    near = _bias_from_dist(ps + tok - off, lambda bk: rbt[:, bk:bk + 1])
    tail_bias = jnp.where(p == last, near, far_bias)
    split = (pps - 1) * ps
    s = jnp.concatenate([s[:, :split] + far_bias, s[:, split:] + tail_bias], axis=1)
    m_old = m_scr[...]
    m_new = jnp.maximum(m_old, jnp.max(s, axis=-1, keepdims=True))
    alpha = jnp.exp(m_old - m_new)
    pr = jnp.exp(s - m_new)
    m_scr[...] = m_new
    l_scr[...] = alpha * l_scr[...] + jnp.sum(pr, axis=-1, keepdims=True)
    pr = pr.astype(BF16)
    acc = alpha * acc_scr[...]
    for i in range(pps):
        spread = (_dot(pr[:, i * ps:(i + 1) * ps], exp_ref[...]) * hmask_ref[...]).astype(BF16)
        acc = acc + _dot(spread, vcat[i * vrows:(i + 1) * vrows, :])
    acc_scr[...] = acc

    @pl.when(p == last)
    def _():
        qf = qblk.astype(F32)
        kn = kn_ref[...].astype(BF16).astype(F32)
        vn = vn_ref[...].astype(BF16).astype(F32)
        vn_rows = [jnp.concatenate(
            [jnp.broadcast_to(vn[tp:tp + 1, hh * DA_VD:(hh + 1) * DA_VD], (2 * SUBLANES, DA_VD))
             for hh in range(DA_HEADS)], axis=0) for tp in range(n_new)]
        tok_c = tok[:, 0:1]
        cols = []
        for tp in range(n_new):
            sc = jnp.sum(qf * kn[tp:tp + 1], axis=-1, keepdims=True)
            dist = tok_c - tp
            bias = rbt[:, n_new - 1:n_new]
            for dd in range(n_new - 2, -1, -1):
                bias = jnp.where(dist == dd, rbt[:, dd:dd + 1], bias)
            cols.append(jnp.where(dist >= 0, sc + bias, NEG))
        m_old = m_scr[...]
        m_new = _tree(jnp.maximum, [m_old] + cols)
        alpha = jnp.exp(m_old - m_new)
        l2 = alpha * l_scr[...]
        acc2 = alpha * acc_scr[...]
        for tp in range(n_new):
            pw = jnp.exp(cols[tp] - m_new)
            l2 = l2 + pw
            acc2 = acc2 + pw.astype(BF16).astype(F32) * vn_rows[tp]
        a = acc2 * (1.0 / l2)
        lam = _lambda(lam_ref, lambda_init)
        outs = []
        for hh in range(DA_HEADS):
            r0 = hh * 2 * SUBLANES
            o = a[r0:r0 + SUBLANES] - lam * a[r0 + SUBLANES:r0 + 2 * SUBLANES]
            o = o * lax.rsqrt(jnp.mean(o * o, axis=-1, keepdims=True) + EPS) * sg_ref[...] * (1.0 - lambda_init)
            outs.append(o)
        o_ref[...] = jnp.concatenate(outs, axis=1)


def _attn_sample(q3, k3, v3, cache_k2, cache_v2, page_table, layer_base, rbt, lamv, subln_g, lambda_init):
    n_dec, n_new, width = k3.shape
    n_pages = page_table.shape[1]
    pps = math.gcd(PAGES_PER_STEP, n_pages)
    assert n_new <= SUBLANES <= _BUCKET_LO[REL_BUCKETS // 2]
    rows = 2 * DA_HEADS * SUBLANES
    vrows = DA_HEADS * PAGE_SIZE
    col = np.arange(vrows)
    spread = jnp.asarray(col[None, :] // DA_HEADS == np.arange(PAGE_SIZE)[:, None], dtype=BF16)
    hmask = jnp.asarray(col[None, :] % DA_HEADS == (np.arange(rows) // (2 * SUBLANES))[:, None], dtype=F32)
    kern = functools.partial(_attn_sample_kernel, pps=pps, n_new=n_new, lambda_init=lambda_init)
    tok_spec = pl.BlockSpec((None, n_new, width), lambda b, p, pt: (b, 0, 0))
    q_spec = pl.BlockSpec((None, rows, width), lambda b, p, pt: (b, 0, 0))
    o_spec = pl.BlockSpec((None, SUBLANES, width), lambda b, p, pt: (b, 0, 0))
    fixed = lambda shape: pl.BlockSpec(shape, lambda b, p, pt: (0,) * len(shape))

    def page_spec(i, cache):
        tail = cache.shape[1:]
        return pl.BlockSpec((None,) + tail,
                            lambda b, p, pt: (layer_base + pt[b * n_pages + p * pps + i],) + (0,) * len(tail))

    grid_spec = pltpu.PrefetchScalarGridSpec(
        num_scalar_prefetch=1,
        grid=(n_dec, n_pages // pps),
        in_specs=[q_spec, tok_spec, tok_spec, fixed(rbt.shape), fixed(lamv.shape), fixed(subln_g.shape),
                  fixed(spread.shape), fixed(hmask.shape)]
        + [page_spec(i, cache_k2) for i in range(pps)] + [page_spec(i, cache_v2) for i in range(pps)],
        out_specs=o_spec,
        scratch_shapes=[pltpu.VMEM((width, pps * PAGE_SIZE), BF16), pltpu.VMEM((pps * vrows, DA_VD), BF16),
                        pltpu.VMEM((rows, 1), F32), pltpu.VMEM((rows, 1), F32), pltpu.VMEM((rows, DA_VD), F32)],
    )
    return pl.pallas_call(
        kern, grid_spec=grid_spec, out_shape=jax.ShapeDtypeStruct((n_dec, SUBLANES, width), F32),
        compiler_params=_params(2),
    )(page_table.reshape(-1), q3, k3, v3, rbt, lamv, subln_g, spread, hmask,
      *([cache_k2] * pps), *([cache_v2] * pps))


def _gdn_kernel(qkv_ref, z_ref, ba_ref, conv0_ref, s0_ref, cw_ref, alog_ref, dtb_ref, gn_ref, tri_ref,
                o_ref, s_ref, ext_scr, *, tb, valid_len):
    blk = pl.program_id(1)
    hk = GDN_HEADS * GDN_DK
    c_len = GDN_CHUNK

    @pl.when(blk == 0)
    def _():
        ext_scr[0:SUBLANES, :] = conv0_ref[...]
        s_ref[...] = s0_ref[...]

    ext_scr[SUBLANES:SUBLANES + tb, :] = qkv_ref[...]
    cw = cw_ref[...]
    y = ext_scr[SUBLANES:SUBLANES + tb, :] * cw[CONV_W - 1:CONV_W]
    for j in range(1, CONV_W):
        y = y + ext_scr[SUBLANES - j:SUBLANES - j + tb, :] * cw[CONV_W - 1 - j:CONV_W - j]
    ext_scr[0:SUBLANES, :] = ext_scr[tb:tb + SUBLANES, :]
    y = y * jax.nn.sigmoid(y)

    ba = ba_ref[...]
    beta_all = jax.nn.sigmoid(ba)
    sp_in = ba + dtb_ref[...]
    softplus = jnp.maximum(sp_in, 0.0) + jnp.log(1.0 + jnp.exp(-jnp.abs(sp_in)))
    g_all = -jnp.exp(alog_ref[...]) * softplus
    if valid_len < tb:
        ok = lax.broadcasted_iota(jnp.int32, (tb, 1), 0) < valid_len
        y = jnp.where(ok, y, 0.0)
        beta_all = jnp.where(ok, beta_all, 0.0)
        g_all = jnp.where(ok, g_all, 0.0)

    ii = lax.broadcasted_iota(jnp.int32, (c_len, c_len), 0)
    jj = lax.broadcasted_iota(jnp.int32, (c_len, c_len), 1)
    eye = (ii == jj).astype(F32)
    z = z_ref[...]
    tri = tri_ref[...].astype(BF16)
    n_chunks = tb // c_len
    units = [(c, h) for c in range(n_chunks) for h in range(GDN_HEADS)]
    rows = lambda c: slice(c * c_len, (c + 1) * c_len)
    gcs = {}
    for c in range(n_chunks):
        gc_all = _dot3((tri,), _split(g_all[rows(c)], 3))
        gcs[c] = (gc_all, gc_all.T)
    st = {}
    for c, h in units:
        rs = rows(c)
        q = y[rs, h * GDN_DK:(h + 1) * GDN_DK]
        k = y[rs, hk + h * GDN_DK:hk + (h + 1) * GDN_DK]
        v = y[rs, 2 * hk + h * GDN_DV:2 * hk + (h + 1) * GDN_DV]
        q = q * lax.rsqrt(jnp.sum(q * q, axis=-1, keepdims=True) + EPS) * (GDN_DK ** -0.5)
        k = k * lax.rsqrt(jnp.sum(k * k, axis=-1, keepdims=True) + EPS)
        beta = beta_all[rs, h:h + 1]
        gc = gcs[c][0][:, GDN_HEADS + h:GDN_HEADS + h + 1]
        gr = gcs[c][1][GDN_HEADS + h:GDN_HEADS + h + 1, :]
        decay = jnp.where(ii >= jj, jnp.exp(jnp.minimum(gc - gr, 0.0)), 0.0)
        kb = k * beta
        k16 = k.astype(BF16)
        egc = jnp.exp(gc)
        g_last = gc[c_len - 1:c_len, :]
        st[c, h] = dict(
            x=-jnp.where(ii > jj, _dot_nt(kb.astype(BF16), k16) * decay, 0.0),
            rhs=_split(jnp.concatenate([v * beta, kb * egc], axis=1)),
            qk=(_dot_nt(q.astype(BF16), k16) * decay).astype(BF16),
            qe=(q * egc).astype(BF16),
            kd_t=(k * jnp.exp(g_last - gc)).T.astype(BF16),
            e_last=jnp.exp(g_last))
    for u_ in units:
        st[u_]["t"] = eye + st[u_]["x"]
        st[u_]["xs"] = _split(st[u_]["x"])
    for _ in range(int(math.log2(c_len)) - 1):
        for u_ in units:
            st[u_]["xs"] = _split(_dot3(st[u_]["xs"], st[u_]["xs"]))
        for u_ in units:
            st[u_]["t"] = st[u_]["t"] + _dot3(_split(st[u_]["t"]), st[u_]["xs"])
    for u_ in units:
        sol = _dot3(_split(st[u_]["t"]), st[u_]["rhs"])
        st[u_]["u"] = sol[:, :GDN_DV]
        st[u_]["w"] = sol[:, GDN_DV:].astype(BF16)
    state = [s_ref[h] for h in range(GDN_HEADS)]
    for c in range(n_chunks):
        for h in range(GDN_HEADS):
            cs = slice(h * GDN_DK, (h + 1) * GDN_DK)
            d = st[c, h]
            s16 = state[h].astype(BF16)
            v_new = d["u"] - _dot(d["w"], s16)
            o = _dot(d["qe"], s16) + _dot(d["qk"], v_new.astype(BF16))
            state[h] = state[h] * d["e_last"] + _dot(d["kd_t"], v_new.astype(BF16))
            o = o * lax.rsqrt(jnp.mean(o * o, axis=-1, keepdims=True) + EPS) * gn_ref[...]
            zz = z[rows(c), cs]
            o_ref[rows(c), cs] = (o * (zz * jax.nn.sigmoid(zz))).astype(o_ref.dtype)
    for h in range(GDN_HEADS):
        s_ref[h] = state[h]


def _gdn(qkv, z, ba, conv0, s0, cw, alog, dtb, gn, tri, n_batch, seq_pad, valid_len):
    tb = min(GDN_TB, seq_pad)
    nblk = seq_pad // tb
    assert valid_len == seq_pad or nblk == 1
    kern = functools.partial(_gdn_kernel, tb=tb, valid_len=min(valid_len, tb))
    row = lambda w: pl.BlockSpec((tb, w), lambda b, i: (b * nblk + i, 0))
    fixed = lambda a: pl.BlockSpec(a.shape, lambda b, i: (0,) * a.ndim)
    state_spec = pl.BlockSpec((None,) + s0.shape[1:], lambda b, i: (b, 0, 0, 0))
    return pl.pallas_call(
        kern,
        grid=(n_batch, nblk),
        in_specs=[row(qkv.shape[1]), row(z.shape[1]), row(ba.shape[1]),
                  pl.BlockSpec((None,) + conv0.shape[1:], lambda b, i: (b, 0, 0)), state_spec,
                  fixed(cw), fixed(alog), fixed(dtb), fixed(gn), fixed(tri)],
        out_specs=[row(z.shape[1]), state_spec],
        out_shape=[jax.ShapeDtypeStruct(z.shape, BF16), jax.ShapeDtypeStruct(s0.shape, F32)],
        scratch_shapes=[pltpu.VMEM((tb + SUBLANES, qkv.shape[1]), F32)],
        compiler_params=_params(2),
    )(qkv, z, ba, conv0, s0, cw, alog, dtb, gn, tri)


def _outproj_kernel(x_ref, oda_ref, og_ref, wo1_ref, wo2_ref, g2_ref, wq_ref, sk_ref, h_ref, hnt_ref, st_ref):
    hres = x_ref[...] + _dot(oda_ref[...].astype(BF16), wo1_ref[...]) + _dot(og_ref[...], wo2_ref[...])
    h_ref[...] = hres
    ms = jnp.mean(hres * hres, axis=-1, keepdims=True)
    hn = hres * lax.rsqrt(ms + EPS) * g2_ref[...]
    hnt_ref[...] = hn.T.astype(BF16)
    q = _dot(hn.astype(BF16), wq_ref[...]).astype(BF16)
    for hs in range(sk_ref.shape[0]):
        st_ref[hs] = _dot_nt(sk_ref[hs], q[:, hs * LANES:(hs + 1) * LANES])


def _outproj(x, oda, og, wo1, wo2, g2, wq, sk):
    t, d = x.shape
    tm = min(PROJ_TM, t)
    row = lambda w: pl.BlockSpec((tm, w), lambda i: (i, 0))
    fixed = lambda a: pl.BlockSpec(a.shape, lambda i: (0,) * a.ndim)
    return pl.pallas_call(
        _outproj_kernel,
        grid=(t // tm,),
        in_specs=[row(d), row(oda.shape[1]), row(og.shape[1])] + [fixed(a) for a in (wo1, wo2, g2, wq, sk)],
        out_specs=[row(d), pl.BlockSpec((d, tm), lambda i: (0, i)),
                   pl.BlockSpec((sk.shape[0], N_KEYS, tm), lambda i: (0, 0, i))],
        out_shape=[jax.ShapeDtypeStruct((t, d), F32), jax.ShapeDtypeStruct((d, t), BF16),
                   jax.ShapeDtypeStruct((sk.shape[0], N_KEYS, t), F32)],
        compiler_params=_params(1),
    )(x, oda, og, wo1, wo2, g2, wq, sk)


def _bitonic_merge_desc(lst):
    n = len(lst)
    j = n // 2
    while j >= 1:
        for i in range(n):
            l = i ^ j
            if l > i:
                lst[i], lst[l] = jnp.maximum(lst[i], lst[l]), jnp.minimum(lst[i], lst[l])
        j //= 2
    return lst


def _bitonic_sort_desc(lst):
    n = len(lst)
    k = 2
    while k <= n:
        j = k // 2
        while j >= 1:
            for i in range(n):
                l = i ^ j
                if l > i:
                    hi, lo = jnp.maximum(lst[i], lst[l]), jnp.minimum(lst[i], lst[l])
                    lst[i], lst[l] = (hi, lo) if (i & k) == 0 else (lo, hi)
            j //= 2
        k *= 2
    return lst


def _tree(fn, xs):
    xs = list(xs)
    while len(xs) > 1:
        xs = [fn(xs[i], xs[i + 1]) if i + 1 < len(xs) else xs[i] for i in range(0, len(xs), 2)]
    return xs[0]


def _peer_topk_kernel(st_ref, ap_ref, cnt_ref, bp_ref, rank_ref):
    kk = PEER_TOPK
    n_vr = N_KEYS // SUBLANES
    tops = []
    for hs in range(2 * PEER_HEADS):
        x = st_ref[hs]
        lst = _bitonic_sort_desc([x[j * SUBLANES:(j + 1) * SUBLANES, :] for j in range(n_vr)])[:kk]
        shift = SUBLANES // 2
        while shift >= 1:
            other = [pltpu.roll(a, shift, 0) for a in lst]
            lst = _bitonic_merge_desc([jnp.maximum(lst[i], other[kk - 1 - i]) for i in range(kk)])
            shift //= 2
        tops.append(lst)

    sub = lax.broadcasted_iota(jnp.int32, tops[0][0].shape, 0)

    def pack(which, a):
        out = tops[which][a]
        for h in range(1, PEER_HEADS):
            out = jnp.where(sub == h, tops[2 * h + which][a], out)
        return out

    v1 = [pack(0, a) for a in range(kk)]
    v2 = [pack(1, a) for a in range(kk)]
    pairs = [(a, b) for a in range(kk) for b in range(kk) if (a + 1) * (b + 1) <= kk]
    cands = [v1[a] + v2[b] for a, b in pairs]
    cur = list(cands)
    for r in range(kk):
        tau = _tree(jnp.maximum, cur)
        if r + 1 < kk:
            cur = [jnp.where(c == tau, -jnp.inf, c) for c in cur]
    top = v1[0] + v2[0]
    zsum = _tree(jnp.add, [jnp.where(c >= tau, jnp.exp(c - top), 0.0) for c in cands])
    rz = 1.0 / zsum
    cnt_of_rank = [_tree(jnp.add, [jnp.where(c >= tau, 1.0, 0.0) for c, (a2, _) in zip(cands, pairs) if a2 == a])
                   for a in range(kk)]
    for h in range(PEER_HEADS):
        row = lambda a, h=h: a[h:h + 1, :]
        s1 = st_ref[2 * h]
        s2 = st_ref[2 * h + 1]
        ap_ref[h] = (jnp.exp(s1 - row(v1[0])) * row(rz)).astype(BF16).astype(F32)
        bp_ref[h] = pltpu.bitcast(jnp.exp(s2 - row(v2[0])).astype(BF16), jnp.uint32)
        cnt = jnp.zeros(s1.shape, F32)
        rank = jnp.full(s2.shape, float(kk), F32)
        for r in range(kk - 1, -1, -1):
            cnt = jnp.where(s1 == row(v1[r]), row(cnt_of_rank[r]), cnt)
            rank = jnp.where(s2 == row(v2[r]), float(r), rank)
        cnt_ref[h] = cnt
        rank_ref[h] = pltpu.bitcast(rank.astype(BF16), jnp.uint32)


def _peer_topk(st):
    n_hs, n_keys, t = st.shape
    blk = pl.BlockSpec((PEER_HEADS, n_keys, LANES), lambda i: (0, 0, i))
    words = n_keys * 2 // 4
    wblk = pl.BlockSpec((PEER_HEADS, words, LANES), lambda i: (0, 0, i))
    f32_out = jax.ShapeDtypeStruct((PEER_HEADS, n_keys, t), F32)
    word_out = jax.ShapeDtypeStruct((PEER_HEADS, words, t), jnp.uint32)
    return pl.pallas_call(
        _peer_topk_kernel,
        grid=(t // LANES,),
        in_specs=[pl.BlockSpec((n_hs, n_keys, LANES), lambda i: (0, 0, i))],
        out_specs=[blk, blk, wblk, wblk],
        out_shape=[f32_out, f32_out, word_out, word_out],
        compiler_params=_params(1),
    )(st)


def _packed_row(row):
    bits = pltpu.bitcast(jnp.broadcast_to(row, (SUBLANES, LANES)), jnp.uint32)
    hi = bits >> 16
    return pltpu.bitcast(hi | (hi << 16), BF16)


def _gelu_tanh(x):
    return 0.5 * x * (1.0 + jnp.tanh(math.sqrt(2.0 / math.pi) * (x + 0.044715 * (x * x * x))))


def _peer_dense_kernel(hnt_ref, ap_ref, cnt_ref, bp_ref, rank_ref, u_ref, vt_ref, h_ref, y_ref, acc_scr, hid_scr,
                       w_scr, *, tb, ec):
    e = pl.program_id(1)
    rows_per_step = ec // N_KEYS
    n_pieces = N_KEYS // BF16_ROWS

    @pl.when(e == 0)
    def _():
        acc_scr[...] = jnp.zeros(acc_scr.shape, F32)

    hid_scr[...] = _dot(u_ref[...], hnt_ref[...])
    for tl in range(tb // LANES):
        ts = slice(tl * LANES, (tl + 1) * LANES)
        for i in range(rows_per_step):
            gates = [jnp.zeros((BF16_ROWS, LANES), BF16)] * n_pieces
            for h in range(PEER_HEADS):
                a_row = _packed_row(ap_ref[h, i:i + 1, ts])
                c_row = _packed_row(cnt_ref[h, i:i + 1, ts])
                for j in range(n_pieces):
                    ws = slice(j * SUBLANES, (j + 1) * SUBLANES)
                    prod = a_row * pltpu.bitcast(bp_ref[h, ws, ts], BF16)
                    keep = pltpu.bitcast(rank_ref[h, ws, ts], BF16) < c_row
                    gates[j] = gates[j] + jnp.where(keep, prod, jnp.zeros_like(prod))
            for j in range(n_pieces):
                rs = slice(i * N_KEYS + j * BF16_ROWS, i * N_KEYS + (j + 1) * BF16_ROWS)
                w_scr[rs, ts] = gates[j] * _gelu_tanh(hid_scr[rs, ts].astype(BF16))
    acc_scr[...] += _dot(vt_ref[...], w_scr[...])

    @pl.when(e == pl.num_programs(1) - 1)
    def _():
        y_ref[...] = h_ref[...] + acc_scr[...].T


def _peer_dense(hnt, ap, cnt, bp, rank, u, vt, hres):
    d, t = hnt.shape
    n_exp = u.shape[0]
    tb = min(PEER_TB, t)
    ec = PEER_EC
    kern = functools.partial(_peer_dense_kernel, tb=tb, ec=ec)
    tok3 = pl.BlockSpec((PEER_HEADS, bp.shape[1], tb), lambda i, e: (0, 0, i))
    rows3 = pl.BlockSpec((PEER_HEADS, ec // N_KEYS, tb), lambda i, e: (0, e, i))
    return pl.pallas_call(
        kern,
        grid=(t // tb, n_exp // ec),
        in_specs=[pl.BlockSpec((d, tb), lambda i, e: (0, i)), rows3, rows3, tok3, tok3,
                  pl.BlockSpec((ec, d), lambda i, e: (e, 0)), pl.BlockSpec((d, ec), lambda i, e: (0, e)),
                  pl.BlockSpec((tb, d), lambda i, e: (i, 0))],
        out_specs=pl.BlockSpec((tb, d), lambda i, e: (i, 0)),
        out_shape=jax.ShapeDtypeStruct((t, d), F32),
        scratch_shapes=[pltpu.VMEM((d, tb), F32), pltpu.VMEM((ec, tb), F32), pltpu.VMEM((ec, tb), BF16)],
        compiler_params=_params(2),
    )(hnt, ap, cnt, bp, rank, u, vt, hres)


def _group_matrix(width, group):
    idx = np.arange(width) // group
    return jnp.asarray(idx[:, None] == idx[None, :], dtype=BF16)


def _pad_lanes(vec, offset, width=LANES):
    return jnp.zeros((1, width), F32).at[0, offset:offset + vec.shape[0]].set(vec)


def _layer_weights(l, p):
    d = p["w_in"].shape[1]
    da_qk = DA_HEADS * 2 * DA_HD
    da_v = DA_HEADS * DA_VD
    g_qkv = GDN_HEADS * (2 * GDN_DK + GDN_DV)
    g_z = GDN_HEADS * GDN_DV
    cuts = np.cumsum([0, da_qk, da_qk, da_v, g_qkv, g_z, GDN_HEADS, GDN_HEADS]).tolist()
    w_in = p["w_in"][l].astype(BF16)
    wba = jnp.zeros((d, LANES), BF16).at[:, :2 * GDN_HEADS].set(w_in[:, cuts[5]:cuts[7]])
    w_out = p["w_out"][l].astype(BF16)
    return dict(
        g1=p["norm1_g"][l][None, :], wqk=w_in[:, cuts[0]:cuts[2]], wv=w_in[:, cuts[2]:cuts[3]],
        wg=w_in[:, cuts[3]:cuts[4]], wz=w_in[:, cuts[4]:cuts[5]], wba=wba,
        gq=jnp.tile(p["da_q_g"][l], 2 * DA_HEADS)[None, :], gk=jnp.tile(p["da_k_g"][l], 2 * DA_HEADS)[None, :],
        grp=_group_matrix(da_qk, DA_HD),
        lamv=jnp.stack([p["lambda_q1"][l], p["lambda_k1"][l], p["lambda_q2"][l], p["lambda_k2"][l]]),
        subln=p["da_subln_g"][l][None, :],
        cw=jnp.zeros((SUBLANES, g_qkv), F32).at[:CONV_W].set(p["gdn_conv_w"][l]),
        alog=_pad_lanes(p["gdn_a_log"][l], GDN_HEADS), dtb=_pad_lanes(p["gdn_dt_bias"][l], GDN_HEADS),
        gn=p["gdn_norm_g"][l][None, :],
        tri=jnp.asarray(np.tril(np.ones((GDN_CHUNK, GDN_CHUNK), np.float32))),
        wo1=w_out[:da_v], wo2=w_out[da_v:], g2=p["norm2_g"][l][None, :],
        wq=p["peer_w_q"][l].astype(BF16),
        sk=p["peer_sub_keys"][l].reshape(2 * PEER_HEADS, N_KEYS, -1).astype(BF16),
        u=p["peer_u"][l].astype(BF16), vt=p["peer_v"][l].T.astype(BF16),
    )


def _mixers_to_output(x2, oda, og, w):
    hres, hnt, st = _outproj(x2, oda, og, w["wo1"], w["wo2"], w["g2"], w["wq"], w["sk"])
    ap, cnt, bp, rank = _peer_topk(st)
    return _peer_dense(hnt, ap, cnt, bp, rank, w["u"], w["vt"], hres)


def kernel(x_prompt, x_sample, cache_k, cache_v, state_ssm, state_conv, page_table, rel_bias, norm1_g, w_in,
           da_q_g, da_k_g, lambda_q1, lambda_k1, lambda_q2, lambda_k2, da_subln_g, gdn_conv_w, gdn_a_log,
           gdn_dt_bias, gdn_norm_g, w_out, norm2_g, peer_w_q, peer_sub_keys, peer_u, peer_v):
    p = dict(norm1_g=norm1_g, w_in=w_in, da_q_g=da_q_g, da_k_g=da_k_g, lambda_q1=lambda_q1, lambda_k1=lambda_k1,
             lambda_q2=lambda_q2, lambda_k2=lambda_k2, da_subln_g=da_subln_g, gdn_conv_w=gdn_conv_w,
             gdn_a_log=gdn_a_log, gdn_dt_bias=gdn_dt_bias, gdn_norm_g=gdn_norm_g, w_out=w_out, norm2_g=norm2_g,
             peer_w_q=peer_w_q, peer_sub_keys=peer_sub_keys, peer_u=peer_u, peer_v=peer_v)
    depth = w_in.shape[0]
    bp_, seq, d = x_prompt.shape
    bd, n_new, _ = x_sample.shape
    n_phys = cache_k.shape[1]
    width = DA_HEADS * DA_VD
    g_qkv = GDN_HEADS * (2 * GDN_DK + GDN_DV)
    cache_k2 = jnp.transpose(cache_k, (0, 1, 3, 4, 5, 2)).reshape(depth * n_phys, DA_HEADS, 2, DA_HD, PAGE_SIZE)
    cache_v2 = cache_v.reshape(depth * n_phys, PAGE_SIZE * DA_HEADS, DA_VD)
    rbt = jnp.repeat(rel_bias.T, 2 * SUBLANES, axis=0)
    pad_s = -(-n_new // GDN_CHUNK) * GDN_CHUNK
    zero_conv = jnp.zeros((bp_, SUBLANES, g_qkv), F32)
    zero_ssm = jnp.zeros((bp_,) + state_ssm.shape[2:], F32)

    yp = x_prompt.reshape(bp_ * seq, d)
    ys = x_sample.reshape(bd * n_new, d)
    outs = [[] for _ in range(8)]
    for l in range(depth):
        lambda_init = 0.8 - 0.6 * math.exp(-0.3 * l)
        w = _layer_weights(l, p)
        proj_w = (w["g1"], w["wqk"], w["wv"], w["wg"], w["wz"], w["wba"], w["gq"], w["gk"], w["grp"])
        gdn_w = (w["cw"], w["alog"], w["dtb"], w["gn"], w["tri"])

        q, k, v, qkv, z, ba = _inproj(yp, *proj_w)
        oda = _attn_prompt(q, k, v, rel_bias, w["lamv"], w["subln"], bp_, seq, lambda_init)
        og, ssm_p = _gdn(qkv, z, ba, zero_conv, zero_ssm, *gdn_w, bp_, seq, seq)
        yp = _mixers_to_output(yp, oda, og, w)
        outs[0].append(k.reshape(bp_, seq, DA_HEADS, 2, DA_HD))
        outs[1].append(v.reshape(bp_, seq, DA_HEADS, DA_VD))
        outs[2].append(ssm_p)
        outs[3].append(qkv.reshape(bp_, seq, g_qkv)[:, seq - (CONV_W - 1):])

        q, k, v, qkv, z, ba = _inproj(ys, *proj_w)
        to3 = lambda a: a.reshape(bd, n_new, a.shape[-1])
        q4 = jnp.pad(q.reshape(bd, n_new, 2 * DA_HEADS, DA_HD), ((0, 0), (0, SUBLANES - n_new), (0, 0), (0, 0)))
        qblk = jnp.einsum("btgd,gf->bgtfd", q4, jnp.eye(2 * DA_HEADS, dtype=F32)).reshape(bd, -1, width)
        oda = _attn_sample(qblk, to3(k), to3(v), cache_k2, cache_v2, page_table, l * n_phys, rbt,
                           w["lamv"], w["subln"], lambda_init)[:, :n_new].reshape(bd * n_new, width)
        pad_tok = lambda a: jnp.pad(to3(a), ((0, 0), (0, pad_s - n_new), (0, 0))).reshape(bd * pad_s, a.shape[-1])
        conv0 = jnp.pad(state_conv[l], ((0, 0), (SUBLANES - (CONV_W - 1), 0), (0, 0)))
        og, ssm_s = _gdn(pad_tok(qkv), pad_tok(z), pad_tok(ba), conv0, state_ssm[l], *gdn_w, bd, pad_s, n_new)
        og = og.reshape(bd, pad_s, -1)[:, :n_new].reshape(bd * n_new, -1)
        ys = _mixers_to_output(ys, oda, og, w)
        conv_all = jnp.concatenate([state_conv[l], to3(qkv)], axis=1)
        outs[4].append(k.reshape(bd, n_new, DA_HEADS, 2, DA_HD))
        outs[5].append(v.reshape(bd, n_new, DA_HEADS, DA_VD))
        outs[6].append(ssm_s)
        outs[7].append(conv_all[:, -(CONV_W - 1):])

    return (yp.reshape(bp_, seq, d), ys.reshape(bd, n_new, d)) + tuple(jnp.stack(o) for o in outs)
```

```python
import functools
import math

import numpy as np
import jax
import jax.numpy as jnp
from jax import lax
from jax.experimental import pallas as pl
from jax.experimental.pallas import tpu as pltpu

F32 = jnp.float32
BF16 = jnp.bfloat16
HI = lax.Precision.HIGHEST

EPS = 1e-6
NEG = -1e30
PAGE_SIZE = 128
DA_HEADS = 4
DA_HD = 64
DA_VD = 2 * DA_HD
GDN_HEADS = 4
GDN_DK = 128
GDN_DV = 128
CONV_W = 4
GDN_CHUNK = 64
REL_BUCKETS = 32
REL_MAX_DIST = 128
PEER_HEADS = 8
N_KEYS = 128
PEER_TOPK = 16
LANES = 128
SUBLANES = 8
BF16_ROWS = 16

PROJ_TM = 256
ATTN_TQ = 512
PAGES_PER_STEP = 8
GDN_TB = 256
PEER_TB = 1024
PEER_EC = 1024
VMEM_LIMIT = 56 * 1024 * 1024


def _dot(a, b, prec=None):
    return jnp.dot(a, b, preferred_element_type=F32, precision=prec)


def _dot_nt(a, b, prec=None):
    return lax.dot_general(a, b, (((1,), (1,)), ((), ())), preferred_element_type=F32, precision=prec)


def _split(a, terms=2):
    parts = []
    for _ in range(terms - 1):
        hi = a.astype(BF16)
        parts.append(hi)
        a = a - hi.astype(F32)
    parts.append(a.astype(BF16))
    return tuple(parts)


def _dot3(a_parts, b_parts):
    n = max(len(a_parts), len(b_parts))
    terms = [_dot(a, b) for i, a in enumerate(a_parts) for j, b in enumerate(b_parts) if i + j < n]
    return functools.reduce(lambda s, t: s + t, terms)


def _params(n_axes, vmem=VMEM_LIMIT):
    return pltpu.CompilerParams(dimension_semantics=("arbitrary",) * n_axes, vmem_limit_bytes=vmem)


def _bucket_lower_bounds():
    n = np.arange(REL_MAX_DIST + 1)
    max_exact = REL_BUCKETS // 2
    nf = np.maximum(n, 1).astype(np.float32)
    large = max_exact + (np.log(nf / max_exact) / math.log(REL_MAX_DIST / max_exact)
                         * (REL_BUCKETS - max_exact)).astype(np.int32)
    bucket = np.where(n < max_exact, n, np.minimum(large, REL_BUCKETS - 1))
    assert bucket[-1] == REL_BUCKETS - 1 and np.all(np.diff(bucket) >= 0)
    return [int(np.argmax(bucket >= b)) for b in range(REL_BUCKETS)]


_BUCKET_LO = _bucket_lower_bounds()


def _bias_from_dist(dist, bias_of_bucket):
    val = bias_of_bucket(REL_BUCKETS - 1)
    for b in range(REL_BUCKETS - 2, -1, -1):
        val = jnp.where(dist < _BUCKET_LO[b + 1], bias_of_bucket(b), val)
    return val


def _group_sum(x2, grp):
    hi = x2.astype(BF16)
    lo = (x2 - hi.astype(F32)).astype(BF16)
    return _dot(hi, grp) + _dot(lo, grp)


def _lambda(lam_ref, lambda_init):
    lv = lam_ref[...]
    t1 = jnp.sum(lv[0:1] * lv[1:2], axis=-1, keepdims=True)
    t2 = jnp.sum(lv[2:3] * lv[3:4], axis=-1, keepdims=True)
    return jnp.exp(t1) - jnp.exp(t2) + lambda_init


def _softmax_update(m, l, acc, s, v):
    m_new = jnp.maximum(m, jnp.max(s, axis=-1, keepdims=True))
    alpha = jnp.exp(m - m_new)
    p = jnp.exp(s - m_new)
    l_new = alpha * l + jnp.sum(p, axis=-1, keepdims=True)
    acc_new = alpha * acc + _dot(p.astype(BF16), v)
    return m_new, l_new, acc_new


def _inproj_kernel(x_ref, g1_ref, wqk_ref, wv_ref, wg_ref, wz_ref, wba_ref, gq_ref, gk_ref, grp_ref,
                   q_ref, k_ref, v_ref, qkv_ref, z_ref, ba_ref):
    x = x_ref[...]
    ms = jnp.mean(x * x, axis=-1, keepdims=True)
    xn = (x * lax.rsqrt(ms + EPS) * g1_ref[...]).astype(BF16)
    qk = _dot(xn, wqk_ref[...])
    half = qk.shape[1] // 2
    grp = grp_ref[...]

    def qk_norm(t, g):
        ms_g = _group_sum(t * t, grp) * (1.0 / DA_HD)
        return t * lax.rsqrt(ms_g + EPS) * g

    q_ref[...] = qk_norm(qk[:, :half], gq_ref[...]) * (DA_HD ** -0.5)
    k_ref[...] = qk_norm(qk[:, half:], gk_ref[...])
    v_ref[...] = _dot(xn, wv_ref[...])
    qkv_ref[...] = _dot(xn, wg_ref[...])
    z_ref[...] = _dot(xn, wz_ref[...])
    ba_ref[...] = _dot(xn, wba_ref[...])


def _inproj(x, g1, wqk, wv, wg, wz, wba, gq, gk, grp):
    t, d = x.shape
    tm = min(PROJ_TM, t)
    row = lambda i: (i, 0)
    fixed = lambda i: (0, 0)
    widths = (wqk.shape[1] // 2, wqk.shape[1] // 2, wv.shape[1], wg.shape[1], wz.shape[1], wba.shape[1])
    return pl.pallas_call(
        _inproj_kernel,
        grid=(t // tm,),
        in_specs=[pl.BlockSpec((tm, d), row), pl.BlockSpec(g1.shape, fixed)]
        + [pl.BlockSpec(w.shape, fixed) for w in (wqk, wv, wg, wz, wba, gq, gk, grp)],
        out_specs=[pl.BlockSpec((tm, w), row) for w in widths],
        out_shape=[jax.ShapeDtypeStruct((t, w), F32) for w in widths],
        compiler_params=_params(1),
    )(x, g1, wqk, wv, wg, wz, wba, gq, gk, grp)


def _attn_prompt_kernel(rb_ref, q_ref, k_ref, v_ref, lam_ref, sg_ref, o_ref, bias_scr, kb_scr, vb_scr,
                        *, tq, lambda_init):
    h = pl.program_id(0)
    b = pl.program_id(1)
    qi = pl.program_id(2)

    @pl.when((b == 0) & (qi == 0))
    def _():
        r = lax.broadcasted_iota(jnp.int32, (tq, tq), 0)
        c = lax.broadcasted_iota(jnp.int32, (tq, tq), 1)
        for which in range(2):
            dist = r - c + which * tq
            val = _bias_from_dist(dist, lambda bk: rb_ref[bk, h])
            bias_scr[which] = jnp.where(dist >= 0, val, NEG)

    @pl.when(qi == 0)
    def _():
        kb_scr[...] = k_ref[...].astype(BF16)
        vb_scr[...] = v_ref[...].astype(BF16)

    q = q_ref[...]
    lane = lax.broadcasted_iota(jnp.int32, q.shape, 1)
    qs = (jnp.where(lane < DA_HD, q, 0.0).astype(BF16), jnp.where(lane >= DA_HD, q, 0.0).astype(BF16))
    far_bias = rb_ref[REL_BUCKETS - 1, h]

    def block(j, bias, carry):
        start = pl.multiple_of(j * tq, tq)
        kblk = kb_scr[pl.ds(start, tq), :]
        vblk = vb_scr[pl.ds(start, tq), :]
        return tuple(_softmax_update(*carry[i], _dot_nt(qs[i], kblk) + bias, vblk) for i in range(2))

    init = (jnp.full((tq, 1), NEG, F32), jnp.zeros((tq, 1), F32), jnp.zeros((tq, DA_VD), F32))
    carry = block(qi, bias_scr[0], (init, init))
    carry = block(jnp.maximum(qi - 1, 0), bias_scr[1] + jnp.where(qi == 0, NEG, 0.0), carry)
    carry = lax.fori_loop(0, jnp.maximum(qi - 1, 0), lambda j, cr: block(j, far_bias, cr), carry)

    (_, l1, acc1), (_, l2, acc2) = carry
    o = acc1 * (1.0 / l1) - _lambda(lam_ref, lambda_init) * (acc2 * (1.0 / l2))
    o = o * lax.rsqrt(jnp.mean(o * o, axis=-1, keepdims=True) + EPS) * sg_ref[...] * (1.0 - lambda_init)
    o_ref[...] = o.astype(o_ref.dtype)


def _attn_prompt(q, k, v, rel_bias, lamv, subln_g, n_batch, seq, lambda_init):
    tq = min(ATTN_TQ, seq)
    nq = seq // tq
    kern = functools.partial(_attn_prompt_kernel, tq=tq, lambda_init=lambda_init)
    kv_spec = pl.BlockSpec((seq, DA_VD), lambda h, b, i: (b, h))
    q_spec = pl.BlockSpec((tq, DA_VD), lambda h, b, i: (b * nq + i, h))
    return pl.pallas_call(
        kern,
        grid=(DA_HEADS, n_batch, nq),
        in_specs=[pl.BlockSpec(memory_space=pltpu.SMEM), q_spec, kv_spec, kv_spec,
                  pl.BlockSpec(lamv.shape, lambda h, b, i: (0, 0)),
                  pl.BlockSpec(subln_g.shape, lambda h, b, i: (0, 0))],
        out_specs=q_spec,
        out_shape=jax.ShapeDtypeStruct(q.shape, BF16),
        scratch_shapes=[pltpu.VMEM((2, tq, tq), F32), pltpu.VMEM((seq, DA_VD), BF16),
                        pltpu.VMEM((seq, DA_VD), BF16)],
        compiler_params=_params(3),
    )(rel_bias, q, k, v, lamv, subln_g)


def _attn_sample_kernel(pt_ref, q_ref, kn_ref, vn_ref, rbt_ref, lam_ref, sg_ref, *rest,
                        pps, n_new, lambda_init):
    del pt_ref
    k_pages = rest[:pps]
    v_pages = rest[pps:2 * pps]
    o_ref, kcat, vcat, m_scr, l_scr, acc_scr = rest[2 * pps:]
    p = pl.program_id(1)
    last = pl.num_programs(1) - 1
    rows = 2 * DA_HEADS * SUBLANES
    ps = PAGE_SIZE

    @pl.when(p == 0)
    def _():
        m_scr[...] = jnp.full(m_scr.shape, NEG, F32)
        l_scr[...] = jnp.zeros(l_scr.shape, F32)
        acc_scr[...] = jnp.zeros(acc_scr.shape, F32)

    qblk = q_ref[...].astype(BF16)
    rbt = rbt_ref[...]
    far_bias = rbt[:, REL_BUCKETS - 1:REL_BUCKETS]

    for i in range(pps):
        for hh in range(DA_HEADS):
            v_head = v_pages[i][pl.ds(hh, ps, stride=DA_HEADS), :]
            vcat[i * ps:(i + 1) * ps, hh * DA_VD:(hh + 1) * DA_VD] = v_head.astype(BF16)
            for sm in range(2):
                c0 = hh * DA_VD + sm * DA_HD
                kcat[c0:c0 + DA_HD, i * ps:(i + 1) * ps] = k_pages[i][hh, sm].astype(BF16)
    s = _dot(qblk, kcat[...])
    tok = lax.broadcasted_iota(jnp.int32, (rows, ps), 0) % SUBLANES
    off = lax.broadcasted_iota(jnp.int32, (rows, ps), 1)
    near = _bias_from_dist(ps + tok - off, lambda bk: rbt[:, bk:bk + 1])
    tail_bias = jnp.where(p == last, near, far_bias)
    split = (pps - 1) * ps
    s = jnp.concatenate([s[:, :split] + far_bias, s[:, split:] + tail_bias], axis=1)
    m_old = m_scr[...]
    m_new = jnp.maximum(m_old, jnp.max(s, axis=-1, keepdims=True))
    alpha = jnp.exp(m_old - m_new)
    pr = jnp.exp(s - m_new)
    m_scr[...] = m_new
    l_scr[...] = alpha * l_scr[...] + jnp.sum(pr, axis=-1, keepdims=True)
    acc_scr[...] = alpha * acc_scr[...] + _dot(pr.astype(BF16), vcat[...])

    @pl.when(p == last)
    def _():
        qf = qblk.astype(F32)
        kn = kn_ref[...].astype(BF16).astype(F32)
        vn = vn_ref[...].astype(BF16).astype(F32)
        tok_c = tok[:, 0:1]
        cols = []
        for tp in range(n_new):
            sc = jnp.sum(qf * kn[tp:tp + 1], axis=-1, keepdims=True)
            dist = tok_c - tp
            bias = rbt[:, n_new - 1:n_new]
            for dd in range(n_new - 2, -1, -1):
                bias = jnp.where(dist == dd, rbt[:, dd:dd + 1], bias)
            cols.append(jnp.where(dist >= 0, sc + bias, NEG))
        m_old = m_scr[...]
        m_new = _tree(jnp.maximum, [m_old] + cols)
        alpha = jnp.exp(m_old - m_new)
        l2 = alpha * l_scr[...]
        acc2 = alpha * acc_scr[...]
        for tp in range(n_new):
            pw = jnp.exp(cols[tp] - m_new)
            l2 = l2 + pw
            acc2 = acc2 + pw.astype(BF16).astype(F32) * vn[tp:tp + 1]
        a = acc2 * (1.0 / l2)
        lam = _lambda(lam_ref, lambda_init)
        outs = []
        for hh in range(DA_HEADS):
            r0 = hh * 2 * SUBLANES
            cols_h = slice(hh * DA_VD, (hh + 1) * DA_VD)
            o = a[r0:r0 + SUBLANES, cols_h] - lam * a[r0 + SUBLANES:r0 + 2 * SUBLANES, cols_h]
            o = o * lax.rsqrt(jnp.mean(o * o, axis=-1, keepdims=True) + EPS) * sg_ref[...] * (1.0 - lambda_init)
            outs.append(o)
        o_ref[...] = jnp.concatenate(outs, axis=1)


def _attn_sample(q3, k3, v3, cache_k2, cache_v2, page_table, layer_base, rbt, lamv, subln_g, lambda_init):
    n_dec, n_new, width = k3.shape
    n_pages = page_table.shape[1]
    pps = math.gcd(PAGES_PER_STEP, n_pages)
    assert n_new <= SUBLANES <= _BUCKET_LO[REL_BUCKETS // 2]
    rows = 2 * DA_HEADS * SUBLANES
    kern = functools.partial(_attn_sample_kernel, pps=pps, n_new=n_new, lambda_init=lambda_init)
    tok_spec = pl.BlockSpec((None, n_new, width), lambda b, p, pt: (b, 0, 0))
    q_spec = pl.BlockSpec((None, rows, width), lambda b, p, pt: (b, 0, 0))
    o_spec = pl.BlockSpec((None, SUBLANES, width), lambda b, p, pt: (b, 0, 0))
    fixed = lambda shape: pl.BlockSpec(shape, lambda b, p, pt: (0,) * len(shape))

    def page_spec(i, cache):
        tail = cache.shape[1:]
        return pl.BlockSpec((None,) + tail,
                            lambda b, p, pt: (layer_base + pt[b * n_pages + p * pps + i],) + (0,) * len(tail))

    grid_spec = pltpu.PrefetchScalarGridSpec(
        num_scalar_prefetch=1,
        grid=(n_dec, n_pages // pps),
        in_specs=[q_spec, tok_spec, tok_spec, fixed(rbt.shape), fixed(lamv.shape), fixed(subln_g.shape)]
        + [page_spec(i, cache_k2) for i in range(pps)] + [page_spec(i, cache_v2) for i in range(pps)],
        out_specs=o_spec,
        scratch_shapes=[pltpu.VMEM((width, pps * PAGE_SIZE), BF16), pltpu.VMEM((pps * PAGE_SIZE, width), BF16),
                        pltpu.VMEM((rows, 1), F32), pltpu.VMEM((rows, 1), F32), pltpu.VMEM((rows, width), F32)],
    )
    return pl.pallas_call(
        kern, grid_spec=grid_spec, out_shape=jax.ShapeDtypeStruct((n_dec, SUBLANES, width), F32),
        compiler_params=_params(2),
    )(page_table.reshape(-1), q3, k3, v3, rbt, lamv, subln_g, *([cache_k2] * pps), *([cache_v2] * pps))


def _gdn_kernel(qkv_ref, z_ref, ba_ref, conv0_ref, s0_ref, cw_ref, alog_ref, dtb_ref, gn_ref, tri_ref,
                o_ref, s_ref, ext_scr, *, tb, valid_len):
    blk = pl.program_id(1)
    hk = GDN_HEADS * GDN_DK
    c_len = GDN_CHUNK

    @pl.when(blk == 0)
    def _():
        ext_scr[0:SUBLANES, :] = conv0_ref[...]
        s_ref[...] = s0_ref[...]

    ext_scr[SUBLANES:SUBLANES + tb, :] = qkv_ref[...]
    cw = cw_ref[...]
    y = ext_scr[SUBLANES:SUBLANES + tb, :] * cw[CONV_W - 1:CONV_W]
    for j in range(1, CONV_W):
        y = y + ext_scr[SUBLANES - j:SUBLANES - j + tb, :] * cw[CONV_W - 1 - j:CONV_W - j]
    ext_scr[0:SUBLANES, :] = ext_scr[tb:tb + SUBLANES, :]
    y = y * jax.nn.sigmoid(y)

    ba = ba_ref[...]
    beta_all = jax.nn.sigmoid(ba)
    sp_in = ba + dtb_ref[...]
    softplus = jnp.maximum(sp_in, 0.0) + jnp.log(1.0 + jnp.exp(-jnp.abs(sp_in)))
    g_all = -jnp.exp(alog_ref[...]) * softplus
    if valid_len < tb:
        ok = lax.broadcasted_iota(jnp.int32, (tb, 1), 0) < valid_len
        y = jnp.where(ok, y, 0.0)
        beta_all = jnp.where(ok, beta_all, 0.0)
        g_all = jnp.where(ok, g_all, 0.0)

    ii = lax.broadcasted_iota(jnp.int32, (c_len, c_len), 0)
    jj = lax.broadcasted_iota(jnp.int32, (c_len, c_len), 1)
    eye = (ii == jj).astype(F32)
    z = z_ref[...]
    tri = tri_ref[...].astype(BF16)
    n_chunks = tb // c_len
    units = [(c, h) for c in range(n_chunks) for h in range(GDN_HEADS)]
    rows = lambda c: slice(c * c_len, (c + 1) * c_len)
    gcs = {}
    for c in range(n_chunks):
        gc_all = _dot3((tri,), _split(g_all[rows(c)], 3))
        gcs[c] = (gc_all, gc_all.T)
    st = {}
    for c, h in units:
        rs = rows(c)
        q = y[rs, h * GDN_DK:(h + 1) * GDN_DK]
        k = y[rs, hk + h * GDN_DK:hk + (h + 1) * GDN_DK]
        v = y[rs, 2 * hk + h * GDN_DV:2 * hk + (h + 1) * GDN_DV]
        q = q * lax.rsqrt(jnp.sum(q * q, axis=-1, keepdims=True) + EPS) * (GDN_DK ** -0.5)
        k = k * lax.rsqrt(jnp.sum(k * k, axis=-1, keepdims=True) + EPS)
        beta = beta_all[rs, h:h + 1]
        gc = gcs[c][0][:, GDN_HEADS + h:GDN_HEADS + h + 1]
        gr = gcs[c][1][GDN_HEADS + h:GDN_HEADS + h + 1, :]
        decay = jnp.where(ii >= jj, jnp.exp(jnp.minimum(gc - gr, 0.0)), 0.0)
        kb = k * beta
        k16 = k.astype(BF16)
        egc = jnp.exp(gc)
        g_last = gc[c_len - 1:c_len, :]
        st[c, h] = dict(
            x=-jnp.where(ii > jj, _dot_nt(kb.astype(BF16), k16) * decay, 0.0),
            rhs=_split(jnp.concatenate([v * beta, kb * egc], axis=1)),
            qk=(_dot_nt(q.astype(BF16), k16) * decay).astype(BF16),
            qe=(q * egc).astype(BF16),
            kd_t=(k * jnp.exp(g_last - gc)).T.astype(BF16),
            e_last=jnp.exp(g_last))
    for u_ in units:
        st[u_]["t"] = eye + st[u_]["x"]
        st[u_]["xs"] = _split(st[u_]["x"])
    for _ in range(int(math.log2(c_len)) - 1):
        for u_ in units:
            st[u_]["xs"] = _split(_dot3(st[u_]["xs"], st[u_]["xs"]))
        for u_ in units:
            st[u_]["t"] = st[u_]["t"] + _dot3(_split(st[u_]["t"]), st[u_]["xs"])
    for u_ in units:
        sol = _dot3(_split(st[u_]["t"]), st[u_]["rhs"])
        st[u_]["u"] = sol[:, :GDN_DV]
        st[u_]["w"] = sol[:, GDN_DV:].astype(BF16)
    state = [s_ref[h] for h in range(GDN_HEADS)]
    for c in range(n_chunks):
        for h in range(GDN_HEADS):
            cs = slice(h * GDN_DK, (h + 1) * GDN_DK)
            d = st[c, h]
            s16 = state[h].astype(BF16)
            v_new = d["u"] - _dot(d["w"], s16)
            o = _dot(d["qe"], s16) + _dot(d["qk"], v_new.astype(BF16))
            state[h] = state[h] * d["e_last"] + _dot(d["kd_t"], v_new.astype(BF16))
            o = o * lax.rsqrt(jnp.mean(o * o, axis=-1, keepdims=True) + EPS) * gn_ref[...]
            zz = z[rows(c), cs]
            o_ref[rows(c), cs] = (o * (zz * jax.nn.sigmoid(zz))).astype(o_ref.dtype)
    for h in range(GDN_HEADS):
        s_ref[h] = state[h]


def _gdn(qkv, z, ba, conv0, s0, cw, alog, dtb, gn, tri, n_batch, seq_pad, valid_len):
    tb = min(GDN_TB, seq_pad)
    nblk = seq_pad // tb
    assert valid_len == seq_pad or nblk == 1
    kern = functools.partial(_gdn_kernel, tb=tb, valid_len=min(valid_len, tb))
    row = lambda w: pl.BlockSpec((tb, w), lambda b, i: (b * nblk + i, 0))
    fixed = lambda a: pl.BlockSpec(a.shape, lambda b, i: (0,) * a.ndim)
    state_spec = pl.BlockSpec((None,) + s0.shape[1:], lambda b, i: (b, 0, 0, 0))
    return pl.pallas_call(
        kern,
        grid=(n_batch, nblk),
        in_specs=[row(qkv.shape[1]), row(z.shape[1]), row(ba.shape[1]),
                  pl.BlockSpec((None,) + conv0.shape[1:], lambda b, i: (b, 0, 0)), state_spec,
                  fixed(cw), fixed(alog), fixed(dtb), fixed(gn), fixed(tri)],
        out_specs=[row(z.shape[1]), state_spec],
        out_shape=[jax.ShapeDtypeStruct(z.shape, BF16), jax.ShapeDtypeStruct(s0.shape, F32)],
        scratch_shapes=[pltpu.VMEM((tb + SUBLANES, qkv.shape[1]), F32)],
        compiler_params=_params(2),
    )(qkv, z, ba, conv0, s0, cw, alog, dtb, gn, tri)


def _outproj_kernel(x_ref, oda_ref, og_ref, wo1_ref, wo2_ref, g2_ref, wq_ref, sk_ref, h_ref, hnt_ref, st_ref):
    hres = x_ref[...] + _dot(oda_ref[...].astype(BF16), wo1_ref[...]) + _dot(og_ref[...], wo2_ref[...])
    h_ref[...] = hres
    ms = jnp.mean(hres * hres, axis=-1, keepdims=True)
    hn = hres * lax.rsqrt(ms + EPS) * g2_ref[...]
    hnt_ref[...] = hn.T.astype(BF16)
    q = _dot(hn.astype(BF16), wq_ref[...]).astype(BF16)
    for hs in range(sk_ref.shape[0]):
        st_ref[hs] = _dot_nt(sk_ref[hs], q[:, hs * LANES:(hs + 1) * LANES])


def _outproj(x, oda, og, wo1, wo2, g2, wq, sk):
    t, d = x.shape
    tm = min(PROJ_TM, t)
    row = lambda w: pl.BlockSpec((tm, w), lambda i: (i, 0))
    fixed = lambda a: pl.BlockSpec(a.shape, lambda i: (0,) * a.ndim)
    return pl.pallas_call(
        _outproj_kernel,
        grid=(t // tm,),
        in_specs=[row(d), row(oda.shape[1]), row(og.shape[1])] + [fixed(a) for a in (wo1, wo2, g2, wq, sk)],
        out_specs=[row(d), pl.BlockSpec((d, tm), lambda i: (0, i)),
                   pl.BlockSpec((sk.shape[0], N_KEYS, tm), lambda i: (0, 0, i))],
        out_shape=[jax.ShapeDtypeStruct((t, d), F32), jax.ShapeDtypeStruct((d, t), BF16),
                   jax.ShapeDtypeStruct((sk.shape[0], N_KEYS, t), F32)],
        compiler_params=_params(1),
    )(x, oda, og, wo1, wo2, g2, wq, sk)


def _bitonic_merge_desc(lst):
    n = len(lst)
    j = n // 2
    while j >= 1:
        for i in range(n):
            l = i ^ j
            if l > i:
                lst[i], lst[l] = jnp.maximum(lst[i], lst[l]), jnp.minimum(lst[i], lst[l])
        j //= 2
    return lst


def _bitonic_sort_desc(lst):
    n = len(lst)
    k = 2
    while k <= n:
        j = k // 2
        while j >= 1:
            for i in range(n):
                l = i ^ j
                if l > i:
                    hi, lo = jnp.maximum(lst[i], lst[l]), jnp.minimum(lst[i], lst[l])
                    lst[i], lst[l] = (hi, lo) if (i & k) == 0 else (lo, hi)
            j //= 2
        k *= 2
    return lst


def _tree(fn, xs):
    xs = list(xs)
    while len(xs) > 1:
        xs = [fn(xs[i], xs[i + 1]) if i + 1 < len(xs) else xs[i] for i in range(0, len(xs), 2)]
    return xs[0]


def _peer_topk_kernel(st_ref, ap_ref, cnt_ref, bp_ref, rank_ref):
    kk = PEER_TOPK
    n_vr = N_KEYS // SUBLANES
    tops = []
    for hs in range(2 * PEER_HEADS):
        x = st_ref[hs]
        lst = _bitonic_sort_desc([x[j * SUBLANES:(j + 1) * SUBLANES, :] for j in range(n_vr)])[:kk]
        shift = SUBLANES // 2
        while shift >= 1:
            other = [pltpu.roll(a, shift, 0) for a in lst]
            lst = _bitonic_merge_desc([jnp.maximum(lst[i], other[kk - 1 - i]) for i in range(kk)])
            shift //= 2
        tops.append(lst)

    sub = lax.broadcasted_iota(jnp.int32, tops[0][0].shape, 0)

    def pack(which, a):
        out = tops[which][a]
        for h in range(1, PEER_HEADS):
            out = jnp.where(sub == h, tops[2 * h + which][a], out)
        return out

    v1 = [pack(0, a) for a in range(kk)]
    v2 = [pack(1, a) for a in range(kk)]
    pairs = [(a, b) for a in range(kk) for b in range(kk) if (a + 1) * (b + 1) <= kk]
    cands = [v1[a] + v2[b] for a, b in pairs]
    cur = list(cands)
    for r in range(kk):
        tau = _tree(jnp.maximum, cur)
        if r + 1 < kk:
            cur = [jnp.where(c == tau, -jnp.inf, c) for c in cur]
    top = v1[0] + v2[0]
    zsum = _tree(jnp.add, [jnp.where(c >= tau, jnp.exp(c - top), 0.0) for c in cands])
    rz = 1.0 / zsum
    cnt_of_rank = [_tree(jnp.add, [jnp.where(c >= tau, 1.0, 0.0) for c, (a2, _) in zip(cands, pairs) if a2 == a])
                   for a in range(kk)]
    for h in range(PEER_HEADS):
        row = lambda a, h=h: a[h:h + 1, :]
        s1 = st_ref[2 * h]
        s2 = st_ref[2 * h + 1]
        ap_ref[h] = (jnp.exp(s1 - row(v1[0])) * row(rz)).astype(BF16).astype(F32)
        bp_ref[h] = pltpu.bitcast(jnp.exp(s2 - row(v2[0])).astype(BF16), jnp.uint32)
        cnt = jnp.zeros(s1.shape, F32)
        rank = jnp.full(s2.shape, float(kk), F32)
        for r in range(kk - 1, -1, -1):
            cnt = jnp.where(s1 == row(v1[r]), row(cnt_of_rank[r]), cnt)
            rank = jnp.where(s2 == row(v2[r]), float(r), rank)
        cnt_ref[h] = cnt
        rank_ref[h] = pltpu.bitcast(rank.astype(BF16), jnp.uint32)


def _peer_topk(st):
    n_hs, n_keys, t = st.shape
    blk = pl.BlockSpec((PEER_HEADS, n_keys, LANES), lambda i: (0, 0, i))
    words = n_keys * 2 // 4
    wblk = pl.BlockSpec((PEER_HEADS, words, LANES), lambda i: (0, 0, i))
    f32_out = jax.ShapeDtypeStruct((PEER_HEADS, n_keys, t), F32)
    word_out = jax.ShapeDtypeStruct((PEER_HEADS, words, t), jnp.uint32)
    return pl.pallas_call(
        _peer_topk_kernel,
        grid=(t // LANES,),
        in_specs=[pl.BlockSpec((n_hs, n_keys, LANES), lambda i: (0, 0, i))],
        out_specs=[blk, blk, wblk, wblk],
        out_shape=[f32_out, f32_out, word_out, word_out],
        compiler_params=_params(1),
    )(st)


def _packed_row(row):
    bits = pltpu.bitcast(jnp.broadcast_to(row, (SUBLANES, LANES)), jnp.uint32)
    hi = bits >> 16
    return pltpu.bitcast(hi | (hi << 16), BF16)


def _gelu_tanh(x):
    return 0.5 * x * (1.0 + jnp.tanh(math.sqrt(2.0 / math.pi) * (x + 0.044715 * (x * x * x))))


def _peer_dense_kernel(hnt_ref, ap_ref, cnt_ref, bp_ref, rank_ref, u_ref, vt_ref, h_ref, y_ref, acc_scr, hid_scr,
                       w_scr, *, tb, ec):
    e = pl.program_id(1)
    rows_per_step = ec // N_KEYS
    n_pieces = N_KEYS // BF16_ROWS

    @pl.when(e == 0)
    def _():
        acc_scr[...] = jnp.zeros(acc_scr.shape, F32)

    hid_scr[...] = _dot(u_ref[...], hnt_ref[...])
    for tl in range(tb // LANES):
        ts = slice(tl * LANES, (tl + 1) * LANES)
        for i in range(rows_per_step):
            gates = [jnp.zeros((BF16_ROWS, LANES), BF16)] * n_pieces
            for h in range(PEER_HEADS):
                a_row = _packed_row(ap_ref[h, i:i + 1, ts])
                c_row = _packed_row(cnt_ref[h, i:i + 1, ts])
                for j in range(n_pieces):
                    ws = slice(j * SUBLANES, (j + 1) * SUBLANES)
                    prod = a_row * pltpu.bitcast(bp_ref[h, ws, ts], BF16)
                    keep = pltpu.bitcast(rank_ref[h, ws, ts], BF16) < c_row
                    gates[j] = gates[j] + jnp.where(keep, prod, jnp.zeros_like(prod))
            for j in range(n_pieces):
                rs = slice(i * N_KEYS + j * BF16_ROWS, i * N_KEYS + (j + 1) * BF16_ROWS)
                w_scr[rs, ts] = gates[j] * _gelu_tanh(hid_scr[rs, ts].astype(BF16))
    acc_scr[...] += _dot(vt_ref[...], w_scr[...])

    @pl.when(e == pl.num_programs(1) - 1)
    def _():
        y_ref[...] = h_ref[...] + acc_scr[...].T


def _peer_dense(hnt, ap, cnt, bp, rank, u, vt, hres):
    d, t = hnt.shape
    n_exp = u.shape[0]
    tb = min(PEER_TB, t)
    ec = PEER_EC
    kern = functools.partial(_peer_dense_kernel, tb=tb, ec=ec)
    tok3 = pl.BlockSpec((PEER_HEADS, bp.shape[1], tb), lambda i, e: (0, 0, i))
    rows3 = pl.BlockSpec((PEER_HEADS, ec // N_KEYS, tb), lambda i, e: (0, e, i))
    return pl.pallas_call(
        kern,
        grid=(t // tb, n_exp // ec),
        in_specs=[pl.BlockSpec((d, tb), lambda i, e: (0, i)), rows3, rows3, tok3, tok3,
                  pl.BlockSpec((ec, d), lambda i, e: (e, 0)), pl.BlockSpec((d, ec), lambda i, e: (0, e)),
                  pl.BlockSpec((tb, d), lambda i, e: (i, 0))],
        out_specs=pl.BlockSpec((tb, d), lambda i, e: (i, 0)),
        out_shape=jax.ShapeDtypeStruct((t, d), F32),
        scratch_shapes=[pltpu.VMEM((d, tb), F32), pltpu.VMEM((ec, tb), F32), pltpu.VMEM((ec, tb), BF16)],
        compiler_params=_params(2),
    )(hnt, ap, cnt, bp, rank, u, vt, hres)


def _group_matrix(width, group):
    idx = np.arange(width) // group
    return jnp.asarray(idx[:, None] == idx[None, :], dtype=BF16)


def _pad_lanes(vec, offset, width=LANES):
    return jnp.zeros((1, width), F32).at[0, offset:offset + vec.shape[0]].set(vec)


def _layer_weights(l, p):
    d = p["w_in"].shape[1]
    da_qk = DA_HEADS * 2 * DA_HD
    da_v = DA_HEADS * DA_VD
    g_qkv = GDN_HEADS * (2 * GDN_DK + GDN_DV)
    g_z = GDN_HEADS * GDN_DV
    cuts = np.cumsum([0, da_qk, da_qk, da_v, g_qkv, g_z, GDN_HEADS, GDN_HEADS]).tolist()
    w_in = p["w_in"][l].astype(BF16)
    wba = jnp.zeros((d, LANES), BF16).at[:, :2 * GDN_HEADS].set(w_in[:, cuts[5]:cuts[7]])
    w_out = p["w_out"][l].astype(BF16)
    return dict(
        g1=p["norm1_g"][l][None, :], wqk=w_in[:, cuts[0]:cuts[2]], wv=w_in[:, cuts[2]:cuts[3]],
        wg=w_in[:, cuts[3]:cuts[4]], wz=w_in[:, cuts[4]:cuts[5]], wba=wba,
        gq=jnp.tile(p["da_q_g"][l], 2 * DA_HEADS)[None, :], gk=jnp.tile(p["da_k_g"][l], 2 * DA_HEADS)[None, :],
        grp=_group_matrix(da_qk, DA_HD),
        lamv=jnp.stack([p["lambda_q1"][l], p["lambda_k1"][l], p["lambda_q2"][l], p["lambda_k2"][l]]),
        subln=p["da_subln_g"][l][None, :],
        cw=jnp.zeros((SUBLANES, g_qkv), F32).at[:CONV_W].set(p["gdn_conv_w"][l]),
        alog=_pad_lanes(p["gdn_a_log"][l], GDN_HEADS), dtb=_pad_lanes(p["gdn_dt_bias"][l], GDN_HEADS),
        gn=p["gdn_norm_g"][l][None, :],
        tri=jnp.asarray(np.tril(np.ones((GDN_CHUNK, GDN_CHUNK), np.float32))),
        wo1=w_out[:da_v], wo2=w_out[da_v:], g2=p["norm2_g"][l][None, :],
        wq=p["peer_w_q"][l].astype(BF16),
        sk=p["peer_sub_keys"][l].reshape(2 * PEER_HEADS, N_KEYS, -1).astype(BF16),
        u=p["peer_u"][l].astype(BF16), vt=p["peer_v"][l].T.astype(BF16),
    )


def _mixers_to_output(x2, oda, og, w):
    hres, hnt, st = _outproj(x2, oda, og, w["wo1"], w["wo2"], w["g2"], w["wq"], w["sk"])
    ap, cnt, bp, rank = _peer_topk(st)
    return _peer_dense(hnt, ap, cnt, bp, rank, w["u"], w["vt"], hres)


def kernel(x_prompt, x_sample, cache_k, cache_v, state_ssm, state_conv, page_table, rel_bias, norm1_g, w_in,
           da_q_g, da_k_g, lambda_q1, lambda_k1, lambda_q2, lambda_k2, da_subln_g, gdn_conv_w, gdn_a_log,
           gdn_dt_bias, gdn_norm_g, w_out, norm2_g, peer_w_q, peer_sub_keys, peer_u, peer_v):
    p = dict(norm1_g=norm1_g, w_in=w_in, da_q_g=da_q_g, da_k_g=da_k_g, lambda_q1=lambda_q1, lambda_k1=lambda_k1,
             lambda_q2=lambda_q2, lambda_k2=lambda_k2, da_subln_g=da_subln_g, gdn_conv_w=gdn_conv_w,
             gdn_a_log=gdn_a_log, gdn_dt_bias=gdn_dt_bias, gdn_norm_g=gdn_norm_g, w_out=w_out, norm2_g=norm2_g,
             peer_w_q=peer_w_q, peer_sub_keys=peer_sub_keys, peer_u=peer_u, peer_v=peer_v)
    depth = w_in.shape[0]
    bp_, seq, d = x_prompt.shape
    bd, n_new, _ = x_sample.shape
    n_phys = cache_k.shape[1]
    width = DA_HEADS * DA_VD
    g_qkv = GDN_HEADS * (2 * GDN_DK + GDN_DV)
    cache_k2 = jnp.transpose(cache_k, (0, 1, 3, 4, 5, 2)).reshape(depth * n_phys, DA_HEADS, 2, DA_HD, PAGE_SIZE)
    cache_v2 = cache_v.reshape(depth * n_phys, PAGE_SIZE * DA_HEADS, DA_VD)
    rbt = jnp.repeat(rel_bias.T, 2 * SUBLANES, axis=0)
    pad_s = -(-n_new // GDN_CHUNK) * GDN_CHUNK
    zero_conv = jnp.zeros((bp_, SUBLANES, g_qkv), F32)
    zero_ssm = jnp.zeros((bp_,) + state_ssm.shape[2:], F32)

    yp = x_prompt.reshape(bp_ * seq, d)
    ys = x_sample.reshape(bd * n_new, d)
    outs = [[] for _ in range(8)]
    for l in range(depth):
        lambda_init = 0.8 - 0.6 * math.exp(-0.3 * l)
        w = _layer_weights(l, p)
        proj_w = (w["g1"], w["wqk"], w["wv"], w["wg"], w["wz"], w["wba"], w["gq"], w["gk"], w["grp"])
        gdn_w = (w["cw"], w["alog"], w["dtb"], w["gn"], w["tri"])

        q, k, v, qkv, z, ba = _inproj(yp, *proj_w)
        oda = _attn_prompt(q, k, v, rel_bias, w["lamv"], w["subln"], bp_, seq, lambda_init)
        og, ssm_p = _gdn(qkv, z, ba, zero_conv, zero_ssm, *gdn_w, bp_, seq, seq)
        yp = _mixers_to_output(yp, oda, og, w)
        outs[0].append(k.reshape(bp_, seq, DA_HEADS, 2, DA_HD))
        outs[1].append(v.reshape(bp_, seq, DA_HEADS, DA_VD))
        outs[2].append(ssm_p)
        outs[3].append(qkv.reshape(bp_, seq, g_qkv)[:, seq - (CONV_W - 1):])

        q, k, v, qkv, z, ba = _inproj(ys, *proj_w)
        to3 = lambda a: a.reshape(bd, n_new, a.shape[-1])
        q4 = jnp.pad(q.reshape(bd, n_new, 2 * DA_HEADS, DA_HD), ((0, 0), (0, SUBLANES - n_new), (0, 0), (0, 0)))
        qblk = jnp.einsum("btgd,gf->bgtfd", q4, jnp.eye(2 * DA_HEADS, dtype=F32)).reshape(bd, -1, width)
        oda = _attn_sample(qblk, to3(k), to3(v), cache_k2, cache_v2, page_table, l * n_phys, rbt,
                           w["lamv"], w["subln"], lambda_init)[:, :n_new].reshape(bd * n_new, width)
        pad_tok = lambda a: jnp.pad(to3(a), ((0, 0), (0, pad_s - n_new), (0, 0))).reshape(bd * pad_s, a.shape[-1])
        conv0 = jnp.pad(state_conv[l], ((0, 0), (SUBLANES - (CONV_W - 1), 0), (0, 0)))
        og, ssm_s = _gdn(pad_tok(qkv), pad_tok(z), pad_tok(ba), conv0, state_ssm[l], *gdn_w, bd, pad_s, n_new)
        og = og.reshape(bd, pad_s, -1)[:, :n_new].reshape(bd * n_new, -1)
        ys = _mixers_to_output(ys, oda, og, w)
        conv_all = jnp.concatenate([state_conv[l], to3(qkv)], axis=1)
        outs[4].append(k.reshape(bd, n_new, DA_HEADS, 2, DA_HD))
        outs[5].append(v.reshape(bd, n_new, DA_HEADS, DA_VD))
        outs[6].append(ssm_s)
        outs[7].append(conv_all[:, -(CONV_W - 1):])

    return (yp.reshape(bp_, seq, d), ys.reshape(bd, n_new, d)) + tuple(jnp.stack(o) for o in outs)
```

```python
import functools
import math

import numpy as np
import jax
import jax.numpy as jnp
from jax import lax
from jax.experimental import pallas as pl
from jax.experimental.pallas import tpu as pltpu

F32 = jnp.float32
BF16 = jnp.bfloat16
HI = lax.Precision.HIGHEST

EPS = 1e-6
NEG = -1e30
PAGE_SIZE = 128
DA_HEADS = 4
DA_HD = 64
DA_VD = 2 * DA_HD
GDN_HEADS = 4
GDN_DK = 128
GDN_DV = 128
CONV_W = 4
GDN_CHUNK = 64
REL_BUCKETS = 32
REL_MAX_DIST = 128
PEER_HEADS = 8
N_KEYS = 128
PEER_TOPK = 16
LANES = 128
SUBLANES = 8
BF16_ROWS = 16

PROJ_TM = 512
ATTN_TQ = 512
PAGES_PER_STEP = 8
GDN_TB = 256
PEER_TB = 1024
PEER_EC = 1024
VMEM_LIMIT = 56 * 1024 * 1024


def _dot(a, b, prec=None):
    return jnp.dot(a, b, preferred_element_type=F32, precision=prec)


def _dot_nt(a, b, prec=None):
    return lax.dot_general(a, b, (((1,), (1,)), ((), ())), preferred_element_type=F32, precision=prec)


def _split(a, terms=2):
    parts = []
    for _ in range(terms - 1):
        hi = a.astype(BF16)
        parts.append(hi)
        a = a - hi.astype(F32)
    parts.append(a.astype(BF16))
    return tuple(parts)


def _dot3(a_parts, b_parts):
    n = max(len(a_parts), len(b_parts))
    terms = [_dot(a, b) for i, a in enumerate(a_parts) for j, b in enumerate(b_parts) if i + j < n]
    return functools.reduce(lambda s, t: s + t, terms)


def _params(n_axes, vmem=VMEM_LIMIT):
    return pltpu.CompilerParams(dimension_semantics=("arbitrary",) * n_axes, vmem_limit_bytes=vmem)


def _bucket_lower_bounds():
    n = np.arange(REL_MAX_DIST + 1)
    max_exact = REL_BUCKETS // 2
    nf = np.maximum(n, 1).astype(np.float32)
    large = max_exact + (np.log(nf / max_exact) / math.log(REL_MAX_DIST / max_exact)
                         * (REL_BUCKETS - max_exact)).astype(np.int32)
    bucket = np.where(n < max_exact, n, np.minimum(large, REL_BUCKETS - 1))
    assert bucket[-1] == REL_BUCKETS - 1 and np.all(np.diff(bucket) >= 0)
    return [int(np.argmax(bucket >= b)) for b in range(REL_BUCKETS)]


_BUCKET_LO = _bucket_lower_bounds()


def _bias_from_dist(dist, bias_of_bucket):
    val = bias_of_bucket(REL_BUCKETS - 1)
    for b in range(REL_BUCKETS - 2, -1, -1):
        val = jnp.where(dist < _BUCKET_LO[b + 1], bias_of_bucket(b), val)
    return val


def _group_sum(x2, grp):
    hi = x2.astype(BF16)
    lo = (x2 - hi.astype(F32)).astype(BF16)
    return _dot(hi, grp) + _dot(lo, grp)


def _lambda(lam_ref, lambda_init):
    lv = lam_ref[...]
    t1 = jnp.sum(lv[0:1] * lv[1:2], axis=-1, keepdims=True)
    t2 = jnp.sum(lv[2:3] * lv[3:4], axis=-1, keepdims=True)
    return jnp.exp(t1) - jnp.exp(t2) + lambda_init


def _inproj_kernel(x_ref, g1_ref, wqk_ref, wv_ref, wg_ref, wz_ref, wba_ref, gq_ref, gk_ref, grp_ref,
                   q_ref, k_ref, v_ref, qkv_ref, z_ref, ba_ref):
    x = x_ref[...]
    ms = jnp.mean(x * x, axis=-1, keepdims=True)
    xn = (x * lax.rsqrt(ms + EPS) * g1_ref[...]).astype(BF16)
    qk = _dot(xn, wqk_ref[...])
    half = qk.shape[1] // 2
    grp = grp_ref[...]

    def qk_norm(t, g):
        ms_g = _group_sum(t * t, grp) * (1.0 / DA_HD)
        return t * lax.rsqrt(ms_g + EPS) * g

    q_ref[...] = qk_norm(qk[:, :half], gq_ref[...]) * (DA_HD ** -0.5)
    k_ref[...] = qk_norm(qk[:, half:], gk_ref[...])
    v_ref[...] = _dot(xn, wv_ref[...])
    qkv_ref[...] = _dot(xn, wg_ref[...])
    z_ref[...] = _dot(xn, wz_ref[...])
    ba_ref[...] = _dot(xn, wba_ref[...])


def _inproj(x, g1, wqk, wv, wg, wz, wba, gq, gk, grp):
    t, d = x.shape
    tm = min(PROJ_TM, t)
    row = lambda i: (i, 0)
    fixed = lambda i: (0, 0)
    widths = (wqk.shape[1] // 2, wqk.shape[1] // 2, wv.shape[1], wg.shape[1], wz.shape[1], wba.shape[1])
    return pl.pallas_call(
        _inproj_kernel,
        grid=(t // tm,),
        in_specs=[pl.BlockSpec((tm, d), row), pl.BlockSpec(g1.shape, fixed)]
        + [pl.BlockSpec(w.shape, fixed) for w in (wqk, wv, wg, wz, wba, gq, gk, grp)],
        out_specs=[pl.BlockSpec((tm, w), row) for w in widths],
        out_shape=[jax.ShapeDtypeStruct((t, w), F32) for w in widths],
        compiler_params=_params(1),
    )(x, g1, wqk, wv, wg, wz, wba, gq, gk, grp)


def _attn_prompt_kernel(rb_ref, q_ref, k_ref, v_ref, lam_ref, sg_ref, o_ref, bias_scr, kb_scr, vb_scr,
                        *, tq, lambda_init):
    h = pl.program_id(0)
    b = pl.program_id(1)
    qi = pl.program_id(2)

    @pl.when((b == 0) & (qi == 0))
    def _():
        r = lax.broadcasted_iota(jnp.int32, (tq, tq), 0)
        c = lax.broadcasted_iota(jnp.int32, (tq, tq), 1)
        for which in range(2):
            dist = r - c + which * tq
            val = _bias_from_dist(dist, lambda bk: rb_ref[bk, h])
            bias_scr[which] = jnp.where(dist >= 0, val, NEG)

    @pl.when(qi == 0)
    def _():
        kb_scr[...] = k_ref[...].astype(BF16)
        vb_scr[...] = v_ref[...].astype(BF16)

    q = q_ref[...]
    lane = lax.broadcasted_iota(jnp.int32, q.shape, 1)
    qs = jnp.concatenate([jnp.where(lane < DA_HD, q, 0.0), jnp.where(lane >= DA_HD, q, 0.0)], axis=0).astype(BF16)
    far_bias = rb_ref[REL_BUCKETS - 1, h]

    def block(j, bias, carry):
        start = pl.multiple_of(j * tq, tq)
        kblk = kb_scr[pl.ds(start, tq), :]
        vblk = vb_scr[pl.ds(start, tq), :]
        s_all = _dot_nt(qs, kblk)
        stats, probs = [], []
        for i in range(2):
            m, l, _ = carry[i]
            s = s_all[i * tq:(i + 1) * tq] + bias
            m_new = jnp.maximum(m, jnp.max(s, axis=-1, keepdims=True))
            alpha = jnp.exp(m - m_new)
            p = jnp.exp(s - m_new)
            stats.append((m_new, alpha * l + jnp.sum(p, axis=-1, keepdims=True), alpha))
            probs.append(p.astype(BF16))
        pv = _dot(jnp.concatenate(probs, axis=0), vblk)
        return tuple((stats[i][0], stats[i][1], stats[i][2] * carry[i][2] + pv[i * tq:(i + 1) * tq])
                     for i in range(2))

    init = (jnp.full((tq, 1), NEG, F32), jnp.zeros((tq, 1), F32), jnp.zeros((tq, DA_VD), F32))
    carry = block(qi, bias_scr[0], (init, init))
    carry = block(jnp.maximum(qi - 1, 0), bias_scr[1] + jnp.where(qi == 0, NEG, 0.0), carry)
    carry = lax.fori_loop(0, jnp.maximum(qi - 1, 0), lambda j, cr: block(j, far_bias, cr), carry)

    (_, l1, acc1), (_, l2, acc2) = carry
    o = acc1 * (1.0 / l1) - _lambda(lam_ref, lambda_init) * (acc2 * (1.0 / l2))
    o = o * lax.rsqrt(jnp.mean(o * o, axis=-1, keepdims=True) + EPS) * sg_ref[...] * (1.0 - lambda_init)
    o_ref[...] = o.astype(o_ref.dtype)


def _attn_prompt(q, k, v, rel_bias, lamv, subln_g, n_batch, seq, lambda_init):
    tq = min(ATTN_TQ, seq)
    nq = seq // tq
    kern = functools.partial(_attn_prompt_kernel, tq=tq, lambda_init=lambda_init)
    kv_spec = pl.BlockSpec((seq, DA_VD), lambda h, b, i: (b, h))
    q_spec = pl.BlockSpec((tq, DA_VD), lambda h, b, i: (b * nq + i, h))
    return pl.pallas_call(
        kern,
        grid=(DA_HEADS, n_batch, nq),
        in_specs=[pl.BlockSpec(memory_space=pltpu.SMEM), q_spec, kv_spec, kv_spec,
                  pl.BlockSpec(lamv.shape, lambda h, b, i: (0, 0)),
                  pl.BlockSpec(subln_g.shape, lambda h, b, i: (0, 0))],
        out_specs=q_spec,
        out_shape=jax.ShapeDtypeStruct(q.shape, BF16),
        scratch_shapes=[pltpu.VMEM((2, tq, tq), F32), pltpu.VMEM((seq, DA_VD), BF16),
                        pltpu.VMEM((seq, DA_VD), BF16)],
        compiler_params=_params(3),
    )(rel_bias, q, k, v, lamv, subln_g)


def _attn_sample_kernel(pt_ref, q_ref, kn_ref, vn_ref, rbt_ref, lam_ref, sg_ref, *rest,
                        pps, n_new, lambda_init):
    del pt_ref
    k_pages = rest[:pps]
    v_pages = rest[pps:2 * pps]
    o_ref, kcat, vcat, m_scr, l_scr, acc_scr = rest[2 * pps:]
    p = pl.program_id(1)
    last = pl.num_programs(1) - 1
    rows = 2 * DA_HEADS * SUBLANES
    ps = PAGE_SIZE

    @pl.when(p == 0)
    def _():
        m_scr[...] = jnp.full(m_scr.shape, NEG, F32)
        l_scr[...] = jnp.zeros(l_scr.shape, F32)
        acc_scr[...] = jnp.zeros(acc_scr.shape, F32)

    qblk = q_ref[...].astype(BF16)
    rbt = rbt_ref[...]
    far_bias = rbt[:, REL_BUCKETS - 1:REL_BUCKETS]

    for i in range(pps):
        for hh in range(DA_HEADS):
            v_head = v_pages[i][pl.ds(hh, ps, stride=DA_HEADS), :]
            vcat[i * ps:(i + 1) * ps, hh * DA_VD:(hh + 1) * DA_VD] = v_head.astype(BF16)
            for sm in range(2):
                c0 = hh * DA_VD + sm * DA_HD
                kcat[c0:c0 + DA_HD, i * ps:(i + 1) * ps] = k_pages[i][hh, sm].astype(BF16)
    s = _dot(qblk, kcat[...])
    tok = lax.broadcasted_iota(jnp.int32, (rows, ps), 0) % SUBLANES
    off = lax.broadcasted_iota(jnp.int32, (rows, ps), 1)
    near = _bias_from_dist(ps + tok - off, lambda bk: rbt[:, bk:bk + 1])
    tail_bias = jnp.where(p == last, near, far_bias)
    split = (pps - 1) * ps
    s = jnp.concatenate([s[:, :split] + far_bias, s[:, split:] + tail_bias], axis=1)
    m_old = m_scr[...]
    m_new = jnp.maximum(m_old, jnp.max(s, axis=-1, keepdims=True))
    alpha = jnp.exp(m_old - m_new)
    pr = jnp.exp(s - m_new)
    m_scr[...] = m_new
    l_scr[...] = alpha * l_scr[...] + jnp.sum(pr, axis=-1, keepdims=True)
    acc_scr[...] = alpha * acc_scr[...] + _dot(pr.astype(BF16), vcat[...])

    @pl.when(p == last)
    def _():
        qf = qblk.astype(F32)
        kn = kn_ref[...].astype(BF16).astype(F32)
        vn = vn_ref[...].astype(BF16).astype(F32)
        tok_c = tok[:, 0:1]
        cols = []
        for tp in range(n_new):
            sc = jnp.sum(qf * kn[tp:tp + 1], axis=-1, keepdims=True)
            dist = tok_c - tp
            bias = rbt[:, n_new - 1:n_new]
            for dd in range(n_new - 2, -1, -1):
                bias = jnp.where(dist == dd, rbt[:, dd:dd + 1], bias)
            cols.append(jnp.where(dist >= 0, sc + bias, NEG))
        m_old = m_scr[...]
        m_new = _tree(jnp.maximum, [m_old] + cols)
        alpha = jnp.exp(m_old - m_new)
        l2 = alpha * l_scr[...]
        acc2 = alpha * acc_scr[...]
        for tp in range(n_new):
            pw = jnp.exp(cols[tp] - m_new)
            l2 = l2 + pw
            acc2 = acc2 + pw.astype(BF16).astype(F32) * vn[tp:tp + 1]
        a = acc2 * (1.0 / l2)
        lam = _lambda(lam_ref, lambda_init)
        outs = []
        for hh in range(DA_HEADS):
            r0 = hh * 2 * SUBLANES
            cols_h = slice(hh * DA_VD, (hh + 1) * DA_VD)
            o = a[r0:r0 + SUBLANES, cols_h] - lam * a[r0 + SUBLANES:r0 + 2 * SUBLANES, cols_h]
            o = o * lax.rsqrt(jnp.mean(o * o, axis=-1, keepdims=True) + EPS) * sg_ref[...] * (1.0 - lambda_init)
            outs.append(o)
        o_ref[...] = jnp.concatenate(outs, axis=1)


def _attn_sample(q3, k3, v3, cache_k2, cache_v2, page_table, layer_base, rbt, lamv, subln_g, lambda_init):
    n_dec, n_new, width = k3.shape
    n_pages = page_table.shape[1]
    pps = math.gcd(PAGES_PER_STEP, n_pages)
    assert n_new <= SUBLANES <= _BUCKET_LO[REL_BUCKETS // 2]
    rows = 2 * DA_HEADS * SUBLANES
    kern = functools.partial(_attn_sample_kernel, pps=pps, n_new=n_new, lambda_init=lambda_init)
    tok_spec = pl.BlockSpec((None, n_new, width), lambda b, p, pt: (b, 0, 0))
    q_spec = pl.BlockSpec((None, rows, width), lambda b, p, pt: (b, 0, 0))
    o_spec = pl.BlockSpec((None, SUBLANES, width), lambda b, p, pt: (b, 0, 0))
    fixed = lambda shape: pl.BlockSpec(shape, lambda b, p, pt: (0,) * len(shape))

    def page_spec(i, cache):
        tail = cache.shape[1:]
        return pl.BlockSpec((None,) + tail,
                            lambda b, p, pt: (layer_base + pt[b * n_pages + p * pps + i],) + (0,) * len(tail))

    grid_spec = pltpu.PrefetchScalarGridSpec(
        num_scalar_prefetch=1,
        grid=(n_dec, n_pages // pps),
        in_specs=[q_spec, tok_spec, tok_spec, fixed(rbt.shape), fixed(lamv.shape), fixed(subln_g.shape)]
        + [page_spec(i, cache_k2) for i in range(pps)] + [page_spec(i, cache_v2) for i in range(pps)],
        out_specs=o_spec,
        scratch_shapes=[pltpu.VMEM((width, pps * PAGE_SIZE), BF16), pltpu.VMEM((pps * PAGE_SIZE, width), BF16),
                        pltpu.VMEM((rows, 1), F32), pltpu.VMEM((rows, 1), F32), pltpu.VMEM((rows, width), F32)],
    )
    return pl.pallas_call(
        kern, grid_spec=grid_spec, out_shape=jax.ShapeDtypeStruct((n_dec, SUBLANES, width), F32),
        compiler_params=_params(2),
    )(page_table.reshape(-1), q3, k3, v3, rbt, lamv, subln_g, *([cache_k2] * pps), *([cache_v2] * pps))


def _gdn_kernel(qkv_ref, z_ref, ba_ref, conv0_ref, s0_ref, cw_ref, alog_ref, dtb_ref, gn_ref, tri_ref,
                o_ref, s_ref, ext_scr, *, tb, valid_len):
    blk = pl.program_id(1)
    hk = GDN_HEADS * GDN_DK
    c_len = GDN_CHUNK

    @pl.when(blk == 0)
    def _():
        ext_scr[0:SUBLANES, :] = conv0_ref[...]
        s_ref[...] = s0_ref[...]

    ext_scr[SUBLANES:SUBLANES + tb, :] = qkv_ref[...]
    cw = cw_ref[...]
    y = ext_scr[SUBLANES:SUBLANES + tb, :] * cw[CONV_W - 1:CONV_W]
    for j in range(1, CONV_W):
        y = y + ext_scr[SUBLANES - j:SUBLANES - j + tb, :] * cw[CONV_W - 1 - j:CONV_W - j]
    ext_scr[0:SUBLANES, :] = ext_scr[tb:tb + SUBLANES, :]
    y = y * jax.nn.sigmoid(y)

    ba = ba_ref[...]
    beta_all = jax.nn.sigmoid(ba)
    sp_in = ba + dtb_ref[...]
    softplus = jnp.maximum(sp_in, 0.0) + jnp.log(1.0 + jnp.exp(-jnp.abs(sp_in)))
    g_all = -jnp.exp(alog_ref[...]) * softplus
    if valid_len < tb:
        ok = lax.broadcasted_iota(jnp.int32, (tb, 1), 0) < valid_len
        y = jnp.where(ok, y, 0.0)
        beta_all = jnp.where(ok, beta_all, 0.0)
        g_all = jnp.where(ok, g_all, 0.0)

    ii = lax.broadcasted_iota(jnp.int32, (c_len, c_len), 0)
    jj = lax.broadcasted_iota(jnp.int32, (c_len, c_len), 1)
    eye = (ii == jj).astype(F32)
    z = z_ref[...]
    tri = tri_ref[...].astype(BF16)
    n_chunks = tb // c_len
    units = [(c, h) for c in range(n_chunks) for h in range(GDN_HEADS)]
    rows = lambda c: slice(c * c_len, (c + 1) * c_len)
    gcs = {}
    for c in range(n_chunks):
        gc_all = _dot3((tri,), _split(g_all[rows(c)], 3))
        gcs[c] = (gc_all, gc_all.T)
    st = {}
    for c, h in units:
        rs = rows(c)
        q = y[rs, h * GDN_DK:(h + 1) * GDN_DK]
        k = y[rs, hk + h * GDN_DK:hk + (h + 1) * GDN_DK]
        v = y[rs, 2 * hk + h * GDN_DV:2 * hk + (h + 1) * GDN_DV]
        q = q * lax.rsqrt(jnp.sum(q * q, axis=-1, keepdims=True) + EPS) * (GDN_DK ** -0.5)
        k = k * lax.rsqrt(jnp.sum(k * k, axis=-1, keepdims=True) + EPS)
        beta = beta_all[rs, h:h + 1]
        gc = gcs[c][0][:, GDN_HEADS + h:GDN_HEADS + h + 1]
        gr = gcs[c][1][GDN_HEADS + h:GDN_HEADS + h + 1, :]
        decay = jnp.where(ii >= jj, jnp.exp(jnp.minimum(gc - gr, 0.0)), 0.0)
        kb = k * beta
        k16 = k.astype(BF16)
        egc = jnp.exp(gc)
        g_last = gc[c_len - 1:c_len, :]
        st[c, h] = dict(
            x=-jnp.where(ii > jj, _dot_nt(kb.astype(BF16), k16) * decay, 0.0),
            rhs=_split(jnp.concatenate([v * beta, kb * egc], axis=1)),
            qk=(_dot_nt(q.astype(BF16), k16) * decay).astype(BF16),
            qe=(q * egc).astype(BF16),
            kd_t=(k * jnp.exp(g_last - gc)).T.astype(BF16),
            e_last=jnp.exp(g_last))
    for u_ in units:
        st[u_]["t"] = eye + st[u_]["x"]
        st[u_]["xs"] = _split(st[u_]["x"])
    for _ in range(int(math.log2(c_len)) - 1):
        for u_ in units:
            st[u_]["xs"] = _split(_dot3(st[u_]["xs"], st[u_]["xs"]))
        for u_ in units:
            st[u_]["t"] = st[u_]["t"] + _dot3(_split(st[u_]["t"]), st[u_]["xs"])
    for u_ in units:
        sol = _dot3(_split(st[u_]["t"]), st[u_]["rhs"])
        st[u_]["u"] = sol[:, :GDN_DV]
        st[u_]["w"] = sol[:, GDN_DV:].astype(BF16)
    state = [s_ref[h] for h in range(GDN_HEADS)]
    for c in range(n_chunks):
        for h in range(GDN_HEADS):
            cs = slice(h * GDN_DK, (h + 1) * GDN_DK)
            d = st[c, h]
            s16 = state[h].astype(BF16)
            v_new = d["u"] - _dot(d["w"], s16)
            o = _dot(d["qe"], s16) + _dot(d["qk"], v_new.astype(BF16))
            state[h] = state[h] * d["e_last"] + _dot(d["kd_t"], v_new.astype(BF16))
            o = o * lax.rsqrt(jnp.mean(o * o, axis=-1, keepdims=True) + EPS) * gn_ref[...]
            zz = z[rows(c), cs]
            o_ref[rows(c), cs] = (o * (zz * jax.nn.sigmoid(zz))).astype(o_ref.dtype)
    for h in range(GDN_HEADS):
        s_ref[h] = state[h]


def _gdn(qkv, z, ba, conv0, s0, cw, alog, dtb, gn, tri, n_batch, seq_pad, valid_len):
    tb = min(GDN_TB, seq_pad)
    nblk = seq_pad // tb
    assert valid_len == seq_pad or nblk == 1
    kern = functools.partial(_gdn_kernel, tb=tb, valid_len=min(valid_len, tb))
    row = lambda w: pl.BlockSpec((tb, w), lambda b, i: (b * nblk + i, 0))
    fixed = lambda a: pl.BlockSpec(a.shape, lambda b, i: (0,) * a.ndim)
    state_spec = pl.BlockSpec((None,) + s0.shape[1:], lambda b, i: (b, 0, 0, 0))
    return pl.pallas_call(
        kern,
        grid=(n_batch, nblk),
        in_specs=[row(qkv.shape[1]), row(z.shape[1]), row(ba.shape[1]),
                  pl.BlockSpec((None,) + conv0.shape[1:], lambda b, i: (b, 0, 0)), state_spec,
                  fixed(cw), fixed(alog), fixed(dtb), fixed(gn), fixed(tri)],
        out_specs=[row(z.shape[1]), state_spec],
        out_shape=[jax.ShapeDtypeStruct(z.shape, BF16), jax.ShapeDtypeStruct(s0.shape, F32)],
        scratch_shapes=[pltpu.VMEM((tb + SUBLANES, qkv.shape[1]), F32)],
        compiler_params=_params(2),
    )(qkv, z, ba, conv0, s0, cw, alog, dtb, gn, tri)


def _outproj_kernel(x_ref, oda_ref, og_ref, wo1_ref, wo2_ref, g2_ref, wq_ref, sk_ref, h_ref, hnt_ref, st_ref):
    hres = x_ref[...] + _dot(oda_ref[...].astype(BF16), wo1_ref[...]) + _dot(og_ref[...], wo2_ref[...])
    h_ref[...] = hres
    ms = jnp.mean(hres * hres, axis=-1, keepdims=True)
    hn = hres * lax.rsqrt(ms + EPS) * g2_ref[...]
    hnt_ref[...] = hn.T.astype(BF16)
    q = _dot(hn.astype(BF16), wq_ref[...]).astype(BF16)
    for hs in range(sk_ref.shape[0]):
        st_ref[hs] = _dot_nt(sk_ref[hs], q[:, hs * LANES:(hs + 1) * LANES])


def _outproj(x, oda, og, wo1, wo2, g2, wq, sk):
    t, d = x.shape
    tm = min(PROJ_TM, t)
    row = lambda w: pl.BlockSpec((tm, w), lambda i: (i, 0))
    fixed = lambda a: pl.BlockSpec(a.shape, lambda i: (0,) * a.ndim)
    return pl.pallas_call(
        _outproj_kernel,
        grid=(t // tm,),
        in_specs=[row(d), row(oda.shape[1]), row(og.shape[1])] + [fixed(a) for a in (wo1, wo2, g2, wq, sk)],
        out_specs=[row(d), pl.BlockSpec((d, tm), lambda i: (0, i)),
                   pl.BlockSpec((sk.shape[0], N_KEYS, tm), lambda i: (0, 0, i))],
        out_shape=[jax.ShapeDtypeStruct((t, d), F32), jax.ShapeDtypeStruct((d, t), BF16),
                   jax.ShapeDtypeStruct((sk.shape[0], N_KEYS, t), F32)],
        compiler_params=_params(1),
    )(x, oda, og, wo1, wo2, g2, wq, sk)


def _bitonic_merge_desc(lst):
    n = len(lst)
    j = n // 2
    while j >= 1:
        for i in range(n):
            l = i ^ j
            if l > i:
                lst[i], lst[l] = jnp.maximum(lst[i], lst[l]), jnp.minimum(lst[i], lst[l])
        j //= 2
    return lst


def _bitonic_sort_desc(lst):
    n = len(lst)
    k = 2
    while k <= n:
        j = k // 2
        while j >= 1:
            for i in range(n):
                l = i ^ j
                if l > i:
                    hi, lo = jnp.maximum(lst[i], lst[l]), jnp.minimum(lst[i], lst[l])
                    lst[i], lst[l] = (hi, lo) if (i & k) == 0 else (lo, hi)
            j //= 2
        k *= 2
    return lst


def _tree(fn, xs):
    xs = list(xs)
    while len(xs) > 1:
        xs = [fn(xs[i], xs[i + 1]) if i + 1 < len(xs) else xs[i] for i in range(0, len(xs), 2)]
    return xs[0]


def _pair_words(x):
    hi = pltpu.bitcast(x, jnp.uint32) >> 16
    return hi | (hi << 16)


def _peer_topk_kernel(st_ref, ap_ref, cnt_ref, bp_ref, rank_ref):
    kk = PEER_TOPK
    n_vr = N_KEYS // SUBLANES
    tops = []
    for hs in range(2 * PEER_HEADS):
        x = st_ref[hs]
        lst = _bitonic_sort_desc([x[j * SUBLANES:(j + 1) * SUBLANES, :] for j in range(n_vr)])[:kk]
        shift = SUBLANES // 2
        while shift >= 1:
            other = [pltpu.roll(a, shift, 0) for a in lst]
            lst = _bitonic_merge_desc([jnp.maximum(lst[i], other[kk - 1 - i]) for i in range(kk)])
            shift //= 2
        tops.append(lst)

    sub = lax.broadcasted_iota(jnp.int32, tops[0][0].shape, 0)

    def pack(which, a):
        out = tops[which][a]
        for h in range(1, PEER_HEADS):
            out = jnp.where(sub == h, tops[2 * h + which][a], out)
        return out

    v1 = [pack(0, a) for a in range(kk)]
    v2 = [pack(1, a) for a in range(kk)]
    pairs = [(a, b) for a in range(kk) for b in range(kk) if (a + 1) * (b + 1) <= kk]
    cands = [v1[a] + v2[b] for a, b in pairs]
    cur = list(cands)
    for r in range(kk):
        tau = _tree(jnp.maximum, cur)
        if r + 1 < kk:
            cur = [jnp.where(c == tau, -jnp.inf, c) for c in cur]
    top = v1[0] + v2[0]
    zsum = _tree(jnp.add, [jnp.where(c >= tau, jnp.exp(c - top), 0.0) for c in cands])
    rz = 1.0 / zsum
    cnt_of_rank = [_tree(jnp.add, [jnp.where(c >= tau, 1.0, 0.0) for c, (a2, _) in zip(cands, pairs) if a2 == a])
                   for a in range(kk)]
    for h in range(PEER_HEADS):
        row = lambda a, h=h: a[h:h + 1, :]
        s1 = st_ref[2 * h]
        s2 = st_ref[2 * h + 1]
        ap_ref[h] = _pair_words((jnp.exp(s1 - row(v1[0])) * row(rz)).astype(BF16).astype(F32))
        bp_ref[h] = pltpu.bitcast(jnp.exp(s2 - row(v2[0])).astype(BF16), jnp.uint32)
        cnt = jnp.zeros(s1.shape, F32)
        rank = jnp.full(s2.shape, float(kk), F32)
        for r in range(kk - 1, -1, -1):
            cnt = jnp.where(s1 == row(v1[r]), row(cnt_of_rank[r]), cnt)
            rank = jnp.where(s2 == row(v2[r]), float(r), rank)
        cnt_ref[h] = _pair_words(cnt)
        rank_ref[h] = pltpu.bitcast(rank.astype(BF16), jnp.uint32)


def _peer_topk(st):
    n_hs, n_keys, t = st.shape
    blk = pl.BlockSpec((PEER_HEADS, n_keys, LANES), lambda i: (0, 0, i))
    words = n_keys * 2 // 4
    wblk = pl.BlockSpec((PEER_HEADS, words, LANES), lambda i: (0, 0, i))
    row_out = jax.ShapeDtypeStruct((PEER_HEADS, n_keys, t), jnp.uint32)
    word_out = jax.ShapeDtypeStruct((PEER_HEADS, words, t), jnp.uint32)
    return pl.pallas_call(
        _peer_topk_kernel,
        grid=(t // LANES,),
        in_specs=[pl.BlockSpec((n_hs, n_keys, LANES), lambda i: (0, 0, i))],
        out_specs=[blk, blk, wblk, wblk],
        out_shape=[row_out, row_out, word_out, word_out],
        compiler_params=_params(1),
    )(st)


def _packed_row(words):
    return pltpu.bitcast(jnp.broadcast_to(words, (SUBLANES, LANES)), BF16)


def _gelu_tanh(x):
    c = math.sqrt(2.0 / math.pi)
    return (0.5 * x) * (1.0 + jnp.tanh(x * (c + (c * 0.044715) * (x * x))))


def _peer_dense_kernel(hnt_ref, ap_ref, cnt_ref, bp_ref, rank_ref, u_ref, vt_ref, h_ref, y_ref, acc_scr, hid_scr,
                       w_scr, *, tb, ec):
    e = pl.program_id(1)
    rows_per_step = ec // N_KEYS
    n_pieces = N_KEYS // BF16_ROWS

    @pl.when(e == 0)
    def _():
        acc_scr[...] = jnp.zeros(acc_scr.shape, F32)

    hid_scr[...] = _dot(u_ref[...], hnt_ref[...])
    for tl in range(tb // LANES):
        ts = slice(tl * LANES, (tl + 1) * LANES)
        for i in range(rows_per_step):
            gates = [jnp.zeros((BF16_ROWS, LANES), BF16)] * n_pieces
            for h in range(PEER_HEADS):
                a_row = _packed_row(ap_ref[h, i:i + 1, ts])
                c_row = _packed_row(cnt_ref[h, i:i + 1, ts])
                for j in range(n_pieces):
                    ws = slice(j * SUBLANES, (j + 1) * SUBLANES)
                    prod = a_row * pltpu.bitcast(bp_ref[h, ws, ts], BF16)
                    keep = pltpu.bitcast(rank_ref[h, ws, ts], BF16) < c_row
                    gates[j] = gates[j] + jnp.where(keep, prod, jnp.zeros_like(prod))
            for j in range(n_pieces):
                rs = slice(i * N_KEYS + j * BF16_ROWS, i * N_KEYS + (j + 1) * BF16_ROWS)
                w_scr[rs, ts] = gates[j] * _gelu_tanh(hid_scr[rs, ts].astype(BF16))
    acc_scr[...] += _dot(vt_ref[...], w_scr[...])

    @pl.when(e == pl.num_programs(1) - 1)
    def _():
        y_ref[...] = h_ref[...] + acc_scr[...].T


def _peer_dense(hnt, ap, cnt, bp, rank, u, vt, hres):
    d, t = hnt.shape
    n_exp = u.shape[0]
    tb = min(PEER_TB, t)
    ec = PEER_EC
    kern = functools.partial(_peer_dense_kernel, tb=tb, ec=ec)
    tok3 = pl.BlockSpec((PEER_HEADS, bp.shape[1], tb), lambda i, e: (0, 0, i))
    rows3 = pl.BlockSpec((PEER_HEADS, ec // N_KEYS, tb), lambda i, e: (0, e, i))
    return pl.pallas_call(
        kern,
        grid=(t // tb, n_exp // ec),
        in_specs=[pl.BlockSpec((d, tb), lambda i, e: (0, i)), rows3, rows3, tok3, tok3,
                  pl.BlockSpec((ec, d), lambda i, e: (e, 0)), pl.BlockSpec((d, ec), lambda i, e: (0, e)),
                  pl.BlockSpec((tb, d), lambda i, e: (i, 0))],
        out_specs=pl.BlockSpec((tb, d), lambda i, e: (i, 0)),
        out_shape=jax.ShapeDtypeStruct((t, d), F32),
        scratch_shapes=[pltpu.VMEM((d, tb), F32), pltpu.VMEM((ec, tb), F32), pltpu.VMEM((ec, tb), BF16)],
        compiler_params=_params(2),
    )(hnt, ap, cnt, bp, rank, u, vt, hres)


def _group_matrix(width, group):
    idx = np.arange(width) // group
    return jnp.asarray(idx[:, None] == idx[None, :], dtype=BF16)


def _pad_lanes(vec, offset, width=LANES):
    return jnp.zeros((1, width), F32).at[0, offset:offset + vec.shape[0]].set(vec)


def _layer_weights(l, p):
    d = p["w_in"].shape[1]
    da_qk = DA_HEADS * 2 * DA_HD
    da_v = DA_HEADS * DA_VD
    g_qkv = GDN_HEADS * (2 * GDN_DK + GDN_DV)
    g_z = GDN_HEADS * GDN_DV
    cuts = np.cumsum([0, da_qk, da_qk, da_v, g_qkv, g_z, GDN_HEADS, GDN_HEADS]).tolist()
    w_in = p["w_in"][l].astype(BF16)
    wba = jnp.zeros((d, LANES), BF16).at[:, :2 * GDN_HEADS].set(w_in[:, cuts[5]:cuts[7]])
    w_out = p["w_out"][l].astype(BF16)
    return dict(
        g1=p["norm1_g"][l][None, :], wqk=w_in[:, cuts[0]:cuts[2]], wv=w_in[:, cuts[2]:cuts[3]],
        wg=w_in[:, cuts[3]:cuts[4]], wz=w_in[:, cuts[4]:cuts[5]], wba=wba,
        gq=jnp.tile(p["da_q_g"][l], 2 * DA_HEADS)[None, :], gk=jnp.tile(p["da_k_g"][l], 2 * DA_HEADS)[None, :],
        grp=_group_matrix(da_qk, DA_HD),
        lamv=jnp.stack([p["lambda_q1"][l], p["lambda_k1"][l], p["lambda_q2"][l], p["lambda_k2"][l]]),
        subln=p["da_subln_g"][l][None, :],
        cw=jnp.zeros((SUBLANES, g_qkv), F32).at[:CONV_W].set(p["gdn_conv_w"][l]),
        alog=_pad_lanes(p["gdn_a_log"][l], GDN_HEADS), dtb=_pad_lanes(p["gdn_dt_bias"][l], GDN_HEADS),
        gn=p["gdn_norm_g"][l][None, :],
        tri=jnp.asarray(np.tril(np.ones((GDN_CHUNK, GDN_CHUNK), np.float32))),
        wo1=w_out[:da_v], wo2=w_out[da_v:], g2=p["norm2_g"][l][None, :],
        wq=p["peer_w_q"][l].astype(BF16),
        sk=p["peer_sub_keys"][l].reshape(2 * PEER_HEADS, N_KEYS, -1).astype(BF16),
        u=p["peer_u"][l].astype(BF16), vt=p["peer_v"][l].T.astype(BF16),
    )


def _mixers_to_output(x2, oda, og, w):
    hres, hnt, st = _outproj(x2, oda, og, w["wo1"], w["wo2"], w["g2"], w["wq"], w["sk"])
    ap, cnt, bp, rank = _peer_topk(st)
    return _peer_dense(hnt, ap, cnt, bp, rank, w["u"], w["vt"], hres)


def kernel(x_prompt, x_sample, cache_k, cache_v, state_ssm, state_conv, page_table, rel_bias, norm1_g, w_in,
           da_q_g, da_k_g, lambda_q1, lambda_k1, lambda_q2, lambda_k2, da_subln_g, gdn_conv_w, gdn_a_log,
           gdn_dt_bias, gdn_norm_g, w_out, norm2_g, peer_w_q, peer_sub_keys, peer_u, peer_v):
    p = dict(norm1_g=norm1_g, w_in=w_in, da_q_g=da_q_g, da_k_g=da_k_g, lambda_q1=lambda_q1, lambda_k1=lambda_k1,
             lambda_q2=lambda_q2, lambda_k2=lambda_k2, da_subln_g=da_subln_g, gdn_conv_w=gdn_conv_w,
             gdn_a_log=gdn_a_log, gdn_dt_bias=gdn_dt_bias, gdn_norm_g=gdn_norm_g, w_out=w_out, norm2_g=norm2_g,
             peer_w_q=peer_w_q, peer_sub_keys=peer_sub_keys, peer_u=peer_u, peer_v=peer_v)
    depth = w_in.shape[0]
    bp_, seq, d = x_prompt.shape
    bd, n_new, _ = x_sample.shape
    n_phys = cache_k.shape[1]
    width = DA_HEADS * DA_VD
    g_qkv = GDN_HEADS * (2 * GDN_DK + GDN_DV)
    cache_k2 = jnp.transpose(cache_k, (0, 1, 3, 4, 5, 2)).reshape(depth * n_phys, DA_HEADS, 2, DA_HD, PAGE_SIZE)
    cache_v2 = cache_v.reshape(depth * n_phys, PAGE_SIZE * DA_HEADS, DA_VD)
    rbt = jnp.repeat(rel_bias.T, 2 * SUBLANES, axis=0)
    pad_s = -(-n_new // GDN_CHUNK) * GDN_CHUNK
    zero_conv = jnp.zeros((bp_, SUBLANES, g_qkv), F32)
    zero_ssm = jnp.zeros((bp_,) + state_ssm.shape[2:], F32)

    yp = x_prompt.reshape(bp_ * seq, d)
    ys = x_sample.reshape(bd * n_new, d)
    outs = [[] for _ in range(8)]
    for l in range(depth):
        lambda_init = 0.8 - 0.6 * math.exp(-0.3 * l)
        w = _layer_weights(l, p)
        proj_w = (w["g1"], w["wqk"], w["wv"], w["wg"], w["wz"], w["wba"], w["gq"], w["gk"], w["grp"])
        gdn_w = (w["cw"], w["alog"], w["dtb"], w["gn"], w["tri"])

        q, k, v, qkv, z, ba = _inproj(yp, *proj_w)
        oda = _attn_prompt(q, k, v, rel_bias, w["lamv"], w["subln"], bp_, seq, lambda_init)
        og, ssm_p = _gdn(qkv, z, ba, zero_conv, zero_ssm, *gdn_w, bp_, seq, seq)
        yp = _mixers_to_output(yp, oda, og, w)
        outs[0].append(k.reshape(bp_, seq, DA_HEADS, 2, DA_HD))
        outs[1].append(v.reshape(bp_, seq, DA_HEADS, DA_VD))
        outs[2].append(ssm_p)
        outs[3].append(qkv.reshape(bp_, seq, g_qkv)[:, seq - (CONV_W - 1):])

        q, k, v, qkv, z, ba = _inproj(ys, *proj_w)
        to3 = lambda a: a.reshape(bd, n_new, a.shape[-1])
        q4 = jnp.pad(q.reshape(bd, n_new, 2 * DA_HEADS, DA_HD), ((0, 0), (0, SUBLANES - n_new), (0, 0), (0, 0)))
        qblk = jnp.einsum("btgd,gf->bgtfd", q4, jnp.eye(2 * DA_HEADS, dtype=F32)).reshape(bd, -1, width)
        oda = _attn_sample(qblk, to3(k), to3(v), cache_k2, cache_v2, page_table, l * n_phys, rbt,
                           w["lamv"], w["subln"], lambda_init)[:, :n_new].reshape(bd * n_new, width)
        pad_tok = lambda a: jnp.pad(to3(a), ((0, 0), (0, pad_s - n_new), (0, 0))).reshape(bd * pad_s, a.shape[-1])
        conv0 = jnp.pad(state_conv[l], ((0, 0), (SUBLANES - (CONV_W - 1), 0), (0, 0)))
        og, ssm_s = _gdn(pad_tok(qkv), pad_tok(z), pad_tok(ba), conv0, state_ssm[l], *gdn_w, bd, pad_s, n_new)
        og = og.reshape(bd, pad_s, -1)[:, :n_new].reshape(bd * n_new, -1)
        ys = _mixers_to_output(ys, oda, og, w)
        conv_all = jnp.concatenate([state_conv[l], to3(qkv)], axis=1)
        outs[4].append(k.reshape(bd, n_new, DA_HEADS, 2, DA_HD))
        outs[5].append(v.reshape(bd, n_new, DA_HEADS, DA_VD))
        outs[6].append(ssm_s)
        outs[7].append(conv_all[:, -(CONV_W - 1):])

    return (yp.reshape(bp_, seq, d), ys.reshape(bd, n_new, d)) + tuple(jnp.stack(o) for o in outs)
```

```python
import functools
import math

import numpy as np
import jax
import jax.numpy as jnp
from jax import lax
from jax.experimental import pallas as pl
from jax.experimental.pallas import tpu as pltpu

F32 = jnp.float32
BF16 = jnp.bfloat16
HI = lax.Precision.HIGHEST

EPS = 1e-6
NEG = -1e30
PAGE_SIZE = 128
DA_HEADS = 4
DA_HD = 64
DA_VD = 2 * DA_HD
GDN_HEADS = 4
GDN_DK = 128
GDN_DV = 128
CONV_W = 4
GDN_CHUNK = 64
REL_BUCKETS = 32
REL_MAX_DIST = 128
PEER_HEADS = 8
N_KEYS = 128
PEER_TOPK = 16
LANES = 128
SUBLANES = 8
BF16_ROWS = 16

PROJ_TM = 512
ATTN_TQ = 512
PAGES_PER_STEP = 16
GDN_TB = 256
PEER_TB = 1024
PEER_EC = 1024
VMEM_LIMIT = 56 * 1024 * 1024


def _dot(a, b, prec=None):
    return jnp.dot(a, b, preferred_element_type=F32, precision=prec)


def _dot_nt(a, b, prec=None):
    return lax.dot_general(a, b, (((1,), (1,)), ((), ())), preferred_element_type=F32, precision=prec)


def _split(a, terms=2):
    parts = []
    for _ in range(terms - 1):
        hi = a.astype(BF16)
        parts.append(hi)
        a = a - hi.astype(F32)
    parts.append(a.astype(BF16))
    return tuple(parts)


def _dot3(a_parts, b_parts):
    n = max(len(a_parts), len(b_parts))
    terms = [_dot(a, b) for i, a in enumerate(a_parts) for j, b in enumerate(b_parts) if i + j < n]
    return functools.reduce(lambda s, t: s + t, terms)


def _params(n_axes, vmem=VMEM_LIMIT):
    return pltpu.CompilerParams(dimension_semantics=("arbitrary",) * n_axes, vmem_limit_bytes=vmem)


def _bucket_lower_bounds():
    n = np.arange(REL_MAX_DIST + 1)
    max_exact = REL_BUCKETS // 2
    nf = np.maximum(n, 1).astype(np.float32)
    large = max_exact + (np.log(nf / max_exact) / math.log(REL_MAX_DIST / max_exact)
                         * (REL_BUCKETS - max_exact)).astype(np.int32)
    bucket = np.where(n < max_exact, n, np.minimum(large, REL_BUCKETS - 1))
    assert bucket[-1] == REL_BUCKETS - 1 and np.all(np.diff(bucket) >= 0)
    return [int(np.argmax(bucket >= b)) for b in range(REL_BUCKETS)]


_BUCKET_LO = _bucket_lower_bounds()


def _bias_from_dist(dist, bias_of_bucket):
    val = bias_of_bucket(REL_BUCKETS - 1)
    for b in range(REL_BUCKETS - 2, -1, -1):
        val = jnp.where(dist < _BUCKET_LO[b + 1], bias_of_bucket(b), val)
    return val


def _group_sum(x2, grp):
    hi = x2.astype(BF16)
    lo = (x2 - hi.astype(F32)).astype(BF16)
    return _dot(hi, grp) + _dot(lo, grp)


def _lambda(lam_ref, lambda_init):
    lv = lam_ref[...]
    t1 = jnp.sum(lv[0:1] * lv[1:2], axis=-1, keepdims=True)
    t2 = jnp.sum(lv[2:3] * lv[3:4], axis=-1, keepdims=True)
    return jnp.exp(t1) - jnp.exp(t2) + lambda_init


def _inproj_kernel(x_ref, g1_ref, wqk_ref, wv_ref, wg_ref, wz_ref, wba_ref, gq_ref, gk_ref, grp_ref,
                   q_ref, k_ref, v_ref, qkv_ref, z_ref, ba_ref):
    x = x_ref[...]
    ms = jnp.mean(x * x, axis=-1, keepdims=True)
    xn = (x * lax.rsqrt(ms + EPS) * g1_ref[...]).astype(BF16)
    qk = _dot(xn, wqk_ref[...])
    half = qk.shape[1] // 2
    grp = grp_ref[...]

    def qk_norm(t, g):
        ms_g = _group_sum(t * t, grp) * (1.0 / DA_HD)
        return t * lax.rsqrt(ms_g + EPS) * g

    q_ref[...] = qk_norm(qk[:, :half], gq_ref[...]) * (DA_HD ** -0.5)
    k_ref[...] = qk_norm(qk[:, half:], gk_ref[...])
    v_ref[...] = _dot(xn, wv_ref[...])
    qkv_ref[...] = _dot(xn, wg_ref[...])
    z_ref[...] = _dot(xn, wz_ref[...])
    ba_ref[...] = _dot(xn, wba_ref[...])


def _inproj(x, g1, wqk, wv, wg, wz, wba, gq, gk, grp):
    t, d = x.shape
    tm = min(PROJ_TM, t)
    row = lambda i: (i, 0)
    fixed = lambda i: (0, 0)
    widths = (wqk.shape[1] // 2, wqk.shape[1] // 2, wv.shape[1], wg.shape[1], wz.shape[1], wba.shape[1])
    return pl.pallas_call(
        _inproj_kernel,
        grid=(t // tm,),
        in_specs=[pl.BlockSpec((tm, d), row), pl.BlockSpec(g1.shape, fixed)]
        + [pl.BlockSpec(w.shape, fixed) for w in (wqk, wv, wg, wz, wba, gq, gk, grp)],
        out_specs=[pl.BlockSpec((tm, w), row) for w in widths],
        out_shape=[jax.ShapeDtypeStruct((t, w), F32) for w in widths],
        compiler_params=_params(1),
    )(x, g1, wqk, wv, wg, wz, wba, gq, gk, grp)


def _attn_prompt_kernel(rb_ref, q_ref, k_ref, v_ref, lam_ref, sg_ref, o_ref, bias_scr, kb_scr, vb_scr,
                        *, tq, lambda_init):
    h = pl.program_id(0)
    b = pl.program_id(1)
    qi = pl.program_id(2)

    @pl.when((b == 0) & (qi == 0))
    def _():
        r = lax.broadcasted_iota(jnp.int32, (tq, tq), 0)
        c = lax.broadcasted_iota(jnp.int32, (tq, tq), 1)
        for which in range(2):
            dist = r - c + which * tq
            val = _bias_from_dist(dist, lambda bk: rb_ref[bk, h])
            bias_scr[which] = jnp.where(dist >= 0, val, NEG)

    @pl.when(qi == 0)
    def _():
        kb_scr[...] = k_ref[...].astype(BF16)
        vb_scr[...] = v_ref[...].astype(BF16)

    q = q_ref[...]
    lane = lax.broadcasted_iota(jnp.int32, q.shape, 1)
    qs = jnp.concatenate([jnp.where(lane < DA_HD, q, 0.0), jnp.where(lane >= DA_HD, q, 0.0)], axis=0).astype(BF16)
    far_bias = rb_ref[REL_BUCKETS - 1, h]

    def block(j, bias, carry):
        start = pl.multiple_of(j * tq, tq)
        kblk = kb_scr[pl.ds(start, tq), :]
        vblk = vb_scr[pl.ds(start, tq), :]
        s_all = _dot_nt(qs, kblk)
        stats, probs = [], []
        for i in range(2):
            m, l, _ = carry[i]
            s = s_all[i * tq:(i + 1) * tq] + bias
            m_new = jnp.maximum(m, jnp.max(s, axis=-1, keepdims=True))
            alpha = jnp.exp(m - m_new)
            p = jnp.exp(s - m_new)
            stats.append((m_new, alpha * l + jnp.sum(p, axis=-1, keepdims=True), alpha))
            probs.append(p.astype(BF16))
        pv = _dot(jnp.concatenate(probs, axis=0), vblk)
        return tuple((stats[i][0], stats[i][1], stats[i][2] * carry[i][2] + pv[i * tq:(i + 1) * tq])
                     for i in range(2))

    init = (jnp.full((tq, 1), NEG, F32), jnp.zeros((tq, 1), F32), jnp.zeros((tq, DA_VD), F32))
    carry = block(qi, bias_scr[0], (init, init))
    carry = block(jnp.maximum(qi - 1, 0), bias_scr[1] + jnp.where(qi == 0, NEG, 0.0), carry)
    carry = lax.fori_loop(0, jnp.maximum(qi - 1, 0), lambda j, cr: block(j, far_bias, cr), carry)

    (_, l1, acc1), (_, l2, acc2) = carry
    o = acc1 * (1.0 / l1) - _lambda(lam_ref, lambda_init) * (acc2 * (1.0 / l2))
    o = o * lax.rsqrt(jnp.mean(o * o, axis=-1, keepdims=True) + EPS) * sg_ref[...] * (1.0 - lambda_init)
    o_ref[...] = o.astype(o_ref.dtype)


def _attn_prompt(q, k, v, rel_bias, lamv, subln_g, n_batch, seq, lambda_init):
    tq = min(ATTN_TQ, seq)
    nq = seq // tq
    kern = functools.partial(_attn_prompt_kernel, tq=tq, lambda_init=lambda_init)
    kv_spec = pl.BlockSpec((seq, DA_VD), lambda h, b, i: (b, h))
    q_spec = pl.BlockSpec((tq, DA_VD), lambda h, b, i: (b * nq + i, h))
    return pl.pallas_call(
        kern,
        grid=(DA_HEADS, n_batch, nq),
        in_specs=[pl.BlockSpec(memory_space=pltpu.SMEM), q_spec, kv_spec, kv_spec,
                  pl.BlockSpec(lamv.shape, lambda h, b, i: (0, 0)),
                  pl.BlockSpec(subln_g.shape, lambda h, b, i: (0, 0))],
        out_specs=q_spec,
        out_shape=jax.ShapeDtypeStruct(q.shape, BF16),
        scratch_shapes=[pltpu.VMEM((2, tq, tq), F32), pltpu.VMEM((seq, DA_VD), BF16),
                        pltpu.VMEM((seq, DA_VD), BF16)],
        compiler_params=_params(3),
    )(rel_bias, q, k, v, lamv, subln_g)


def _attn_sample_kernel(pt_ref, q_ref, kn_ref, vn_ref, rbt_ref, lam_ref, sg_ref, *rest,
                        pps, n_new, lambda_init):
    del pt_ref
    k_pages = rest[:pps]
    v_pages = rest[pps:2 * pps]
    o_ref, kcat, vcat, m_scr, l_scr, acc_scr = rest[2 * pps:]
    p = pl.program_id(1)
    last = pl.num_programs(1) - 1
    rows = 2 * DA_HEADS * SUBLANES
    ps = PAGE_SIZE

    @pl.when(p == 0)
    def _():
        m_scr[...] = jnp.full(m_scr.shape, NEG, F32)
        l_scr[...] = jnp.zeros(l_scr.shape, F32)
        acc_scr[...] = jnp.zeros(acc_scr.shape, F32)

    qblk = q_ref[...].astype(BF16)
    rbt = rbt_ref[...]
    far_bias = rbt[:, REL_BUCKETS - 1:REL_BUCKETS]

    for i in range(pps):
        for hh in range(DA_HEADS):
            v_head = v_pages[i][pl.ds(hh, ps, stride=DA_HEADS), :]
            vcat[i * ps:(i + 1) * ps, hh * DA_VD:(hh + 1) * DA_VD] = v_head.astype(BF16)
            for sm in range(2):
                c0 = hh * DA_VD + sm * DA_HD
                kcat[c0:c0 + DA_HD, i * ps:(i + 1) * ps] = k_pages[i][hh, sm].astype(BF16)
    s = _dot(qblk, kcat[...])
    tok = lax.broadcasted_iota(jnp.int32, (rows, ps), 0) % SUBLANES
    off = lax.broadcasted_iota(jnp.int32, (rows, ps), 1)
    near = _bias_from_dist(ps + tok - off, lambda bk: rbt[:, bk:bk + 1])
    tail_bias = jnp.where(p == last, near, far_bias)
    split = (pps - 1) * ps
    s = jnp.concatenate([s[:, :split] + far_bias, s[:, split:] + tail_bias], axis=1)
    m_old = m_scr[...]
    m_new = jnp.maximum(m_old, jnp.max(s, axis=-1, keepdims=True))
    alpha = jnp.exp(m_old - m_new)
    pr = jnp.exp(s - m_new)
    m_scr[...] = m_new
    l_scr[...] = alpha * l_scr[...] + jnp.sum(pr, axis=-1, keepdims=True)
    acc_scr[...] = alpha * acc_scr[...] + _dot(pr.astype(BF16), vcat[...])

    @pl.when(p == last)
    def _():
        qf = qblk.astype(F32)
        kn = kn_ref[...].astype(BF16).astype(F32)
        vn = vn_ref[...].astype(BF16).astype(F32)
        tok_c = tok[:, 0:1]
        cols = []
        for tp in range(n_new):
            sc = jnp.sum(qf * kn[tp:tp + 1], axis=-1, keepdims=True)
            dist = tok_c - tp
            bias = rbt[:, n_new - 1:n_new]
            for dd in range(n_new - 2, -1, -1):
                bias = jnp.where(dist == dd, rbt[:, dd:dd + 1], bias)
            cols.append(jnp.where(dist >= 0, sc + bias, NEG))
        m_old = m_scr[...]
        m_new = _tree(jnp.maximum, [m_old] + cols)
        alpha = jnp.exp(m_old - m_new)
        l2 = alpha * l_scr[...]
        acc2 = alpha * acc_scr[...]
        for tp in range(n_new):
            pw = jnp.exp(cols[tp] - m_new)
            l2 = l2 + pw
            acc2 = acc2 + pw.astype(BF16).astype(F32) * vn[tp:tp + 1]
        a = acc2 * (1.0 / l2)
        lam = _lambda(lam_ref, lambda_init)
        outs = []
        for hh in range(DA_HEADS):
            r0 = hh * 2 * SUBLANES
            cols_h = slice(hh * DA_VD, (hh + 1) * DA_VD)
            o = a[r0:r0 + SUBLANES, cols_h] - lam * a[r0 + SUBLANES:r0 + 2 * SUBLANES, cols_h]
            o = o * lax.rsqrt(jnp.mean(o * o, axis=-1, keepdims=True) + EPS) * sg_ref[...] * (1.0 - lambda_init)
            outs.append(o)
        o_ref[...] = jnp.concatenate(outs, axis=1)


def _attn_sample(q3, k3, v3, cache_k2, cache_v2, page_table, layer_base, rbt, lamv, subln_g, lambda_init):
    n_dec, n_new, width = k3.shape
    n_pages = page_table.shape[1]
    pps = math.gcd(PAGES_PER_STEP, n_pages)
    assert n_new <= SUBLANES <= _BUCKET_LO[REL_BUCKETS // 2]
    rows = 2 * DA_HEADS * SUBLANES
    kern = functools.partial(_attn_sample_kernel, pps=pps, n_new=n_new, lambda_init=lambda_init)
    tok_spec = pl.BlockSpec((None, n_new, width), lambda b, p, pt: (b, 0, 0))
    q_spec = pl.BlockSpec((None, rows, width), lambda b, p, pt: (b, 0, 0))
    o_spec = pl.BlockSpec((None, SUBLANES, width), lambda b, p, pt: (b, 0, 0))
    fixed = lambda shape: pl.BlockSpec(shape, lambda b, p, pt: (0,) * len(shape))

    def page_spec(i, cache):
        tail = cache.shape[1:]
        return pl.BlockSpec((None,) + tail,
                            lambda b, p, pt: (layer_base + pt[b * n_pages + p * pps + i],) + (0,) * len(tail))

    grid_spec = pltpu.PrefetchScalarGridSpec(
        num_scalar_prefetch=1,
        grid=(n_dec, n_pages // pps),
        in_specs=[q_spec, tok_spec, tok_spec, fixed(rbt.shape), fixed(lamv.shape), fixed(subln_g.shape)]
        + [page_spec(i, cache_k2) for i in range(pps)] + [page_spec(i, cache_v2) for i in range(pps)],
        out_specs=o_spec,
        scratch_shapes=[pltpu.VMEM((width, pps * PAGE_SIZE), BF16), pltpu.VMEM((pps * PAGE_SIZE, width), BF16),
                        pltpu.VMEM((rows, 1), F32), pltpu.VMEM((rows, 1), F32), pltpu.VMEM((rows, width), F32)],
    )
    return pl.pallas_call(
        kern, grid_spec=grid_spec, out_shape=jax.ShapeDtypeStruct((n_dec, SUBLANES, width), F32),
        compiler_params=_params(2),
    )(page_table.reshape(-1), q3, k3, v3, rbt, lamv, subln_g, *([cache_k2] * pps), *([cache_v2] * pps))


def _gdn_kernel(qkv_ref, z_ref, ba_ref, conv0_ref, s0_ref, cw_ref, alog_ref, dtb_ref, gn_ref, tri_ref,
                o_ref, s_ref, ext_scr, *, tb, valid_len):
    blk = pl.program_id(1)
    hk = GDN_HEADS * GDN_DK
    c_len = GDN_CHUNK

    @pl.when(blk == 0)
    def _():
        ext_scr[0:SUBLANES, :] = conv0_ref[...]
        s_ref[...] = s0_ref[...]

    ext_scr[SUBLANES:SUBLANES + tb, :] = qkv_ref[...]
    cw = cw_ref[...]
    y = ext_scr[SUBLANES:SUBLANES + tb, :] * cw[CONV_W - 1:CONV_W]
    for j in range(1, CONV_W):
        y = y + ext_scr[SUBLANES - j:SUBLANES - j + tb, :] * cw[CONV_W - 1 - j:CONV_W - j]
    ext_scr[0:SUBLANES, :] = ext_scr[tb:tb + SUBLANES, :]
    y = y * jax.nn.sigmoid(y)

    ba = ba_ref[...]
    beta_all = jax.nn.sigmoid(ba)
    sp_in = ba + dtb_ref[...]
    softplus = jnp.maximum(sp_in, 0.0) + jnp.log(1.0 + jnp.exp(-jnp.abs(sp_in)))
    g_all = -jnp.exp(alog_ref[...]) * softplus
    if valid_len < tb:
        ok = lax.broadcasted_iota(jnp.int32, (tb, 1), 0) < valid_len
        y = jnp.where(ok, y, 0.0)
        beta_all = jnp.where(ok, beta_all, 0.0)
        g_all = jnp.where(ok, g_all, 0.0)

    ii = lax.broadcasted_iota(jnp.int32, (c_len, c_len), 0)
    jj = lax.broadcasted_iota(jnp.int32, (c_len, c_len), 1)
    eye = (ii == jj).astype(F32)
    z = z_ref[...]
    tri = tri_ref[...].astype(BF16)
    n_chunks = tb // c_len
    units = [(c, h) for c in range(n_chunks) for h in range(GDN_HEADS)]
    rows = lambda c: slice(c * c_len, (c + 1) * c_len)
    gcs = {}
    for c in range(n_chunks):
        gc_all = _dot3((tri,), _split(g_all[rows(c)], 3))
        gcs[c] = (gc_all, gc_all.T)
    st = {}
    for c, h in units:
        rs = rows(c)
        q = y[rs, h * GDN_DK:(h + 1) * GDN_DK]
        k = y[rs, hk + h * GDN_DK:hk + (h + 1) * GDN_DK]
        v = y[rs, 2 * hk + h * GDN_DV:2 * hk + (h + 1) * GDN_DV]
        q = q * lax.rsqrt(jnp.sum(q * q, axis=-1, keepdims=True) + EPS) * (GDN_DK ** -0.5)
        k = k * lax.rsqrt(jnp.sum(k * k, axis=-1, keepdims=True) + EPS)
        beta = beta_all[rs, h:h + 1]
        gc = gcs[c][0][:, GDN_HEADS + h:GDN_HEADS + h + 1]
        gr = gcs[c][1][GDN_HEADS + h:GDN_HEADS + h + 1, :]
        decay = jnp.where(ii >= jj, jnp.exp(jnp.minimum(gc - gr, 0.0)), 0.0)
        kb = k * beta
        k16 = k.astype(BF16)
        egc = jnp.exp(gc)
        g_last = gc[c_len - 1:c_len, :]
        st[c, h] = dict(
            x=-jnp.where(ii > jj, _dot_nt(kb.astype(BF16), k16) * decay, 0.0),
            rhs=_split(jnp.concatenate([v * beta, kb * egc], axis=1)),
            qk=(_dot_nt(q.astype(BF16), k16) * decay).astype(BF16),
            qe=(q * egc).astype(BF16),
            kd_t=(k * jnp.exp(g_last - gc)).T.astype(BF16),
            e_last=jnp.exp(g_last))
    for u_ in units:
        st[u_]["t"] = eye + st[u_]["x"]
        st[u_]["xs"] = _split(st[u_]["x"])
    for _ in range(int(math.log2(c_len)) - 1):
        for u_ in units:
            st[u_]["xs"] = _split(_dot3(st[u_]["xs"], st[u_]["xs"]))
        for u_ in units:
            st[u_]["t"] = st[u_]["t"] + _dot3(_split(st[u_]["t"]), st[u_]["xs"])
    for u_ in units:
        sol = _dot3(_split(st[u_]["t"]), st[u_]["rhs"])
        st[u_]["u"] = sol[:, :GDN_DV]
        st[u_]["w"] = sol[:, GDN_DV:].astype(BF16)
    state = [s_ref[h] for h in range(GDN_HEADS)]
    for c in range(n_chunks):
        for h in range(GDN_HEADS):
            cs = slice(h * GDN_DK, (h + 1) * GDN_DK)
            d = st[c, h]
            s16 = state[h].astype(BF16)
            v_new = d["u"] - _dot(d["w"], s16)
            o = _dot(d["qe"], s16) + _dot(d["qk"], v_new.astype(BF16))
            state[h] = state[h] * d["e_last"] + _dot(d["kd_t"], v_new.astype(BF16))
            o = o * lax.rsqrt(jnp.mean(o * o, axis=-1, keepdims=True) + EPS) * gn_ref[...]
            zz = z[rows(c), cs]
            o_ref[rows(c), cs] = (o * (zz * jax.nn.sigmoid(zz))).astype(o_ref.dtype)
    for h in range(GDN_HEADS):
        s_ref[h] = state[h]


def _gdn(qkv, z, ba, conv0, s0, cw, alog, dtb, gn, tri, n_batch, seq_pad, valid_len):
    tb = min(GDN_TB, seq_pad)
    nblk = seq_pad // tb
    assert valid_len == seq_pad or nblk == 1
    kern = functools.partial(_gdn_kernel, tb=tb, valid_len=min(valid_len, tb))
    row = lambda w: pl.BlockSpec((tb, w), lambda b, i: (b * nblk + i, 0))
    fixed = lambda a: pl.BlockSpec(a.shape, lambda b, i: (0,) * a.ndim)
    state_spec = pl.BlockSpec((None,) + s0.shape[1:], lambda b, i: (b, 0, 0, 0))
    return pl.pallas_call(
        kern,
        grid=(n_batch, nblk),
        in_specs=[row(qkv.shape[1]), row(z.shape[1]), row(ba.shape[1]),
                  pl.BlockSpec((None,) + conv0.shape[1:], lambda b, i: (b, 0, 0)), state_spec,
                  fixed(cw), fixed(alog), fixed(dtb), fixed(gn), fixed(tri)],
        out_specs=[row(z.shape[1]), state_spec],
        out_shape=[jax.ShapeDtypeStruct(z.shape, BF16), jax.ShapeDtypeStruct(s0.shape, F32)],
        scratch_shapes=[pltpu.VMEM((tb + SUBLANES, qkv.shape[1]), F32)],
        compiler_params=_params(2),
    )(qkv, z, ba, conv0, s0, cw, alog, dtb, gn, tri)


def _outproj_kernel(x_ref, oda_ref, og_ref, wo1_ref, wo2_ref, g2_ref, wq_ref, sk_ref, h_ref, hnt_ref, st_ref):
    hres = x_ref[...] + _dot(oda_ref[...].astype(BF16), wo1_ref[...]) + _dot(og_ref[...], wo2_ref[...])
    h_ref[...] = hres
    ms = jnp.mean(hres * hres, axis=-1, keepdims=True)
    hn = hres * lax.rsqrt(ms + EPS) * g2_ref[...]
    hnt_ref[...] = hn.T.astype(BF16)
    q = _dot(hn.astype(BF16), wq_ref[...]).astype(BF16)
    for hs in range(sk_ref.shape[0]):
        st_ref[hs] = _dot_nt(sk_ref[hs], q[:, hs * LANES:(hs + 1) * LANES])


def _outproj(x, oda, og, wo1, wo2, g2, wq, sk):
    t, d = x.shape
    tm = min(PROJ_TM, t)
    row = lambda w: pl.BlockSpec((tm, w), lambda i: (i, 0))
    fixed = lambda a: pl.BlockSpec(a.shape, lambda i: (0,) * a.ndim)
    return pl.pallas_call(
        _outproj_kernel,
        grid=(t // tm,),
        in_specs=[row(d), row(oda.shape[1]), row(og.shape[1])] + [fixed(a) for a in (wo1, wo2, g2, wq, sk)],
        out_specs=[row(d), pl.BlockSpec((d, tm), lambda i: (0, i)),
                   pl.BlockSpec((sk.shape[0], N_KEYS, tm), lambda i: (0, 0, i))],
        out_shape=[jax.ShapeDtypeStruct((t, d), F32), jax.ShapeDtypeStruct((d, t), BF16),
                   jax.ShapeDtypeStruct((sk.shape[0], N_KEYS, t), F32)],
        compiler_params=_params(1),
    )(x, oda, og, wo1, wo2, g2, wq, sk)


def _bitonic_merge_desc(lst):
    n = len(lst)
    j = n // 2
    while j >= 1:
        for i in range(n):
            l = i ^ j
            if l > i:
                lst[i], lst[l] = jnp.maximum(lst[i], lst[l]), jnp.minimum(lst[i], lst[l])
        j //= 2
    return lst


def _bitonic_sort_desc(lst):
    n = len(lst)
    k = 2
    while k <= n:
        j = k // 2
        while j >= 1:
            for i in range(n):
                l = i ^ j
                if l > i:
                    hi, lo = jnp.maximum(lst[i], lst[l]), jnp.minimum(lst[i], lst[l])
                    lst[i], lst[l] = (hi, lo) if (i & k) == 0 else (lo, hi)
            j //= 2
        k *= 2
    return lst


def _tree(fn, xs):
    xs = list(xs)
    while len(xs) > 1:
        xs = [fn(xs[i], xs[i + 1]) if i + 1 < len(xs) else xs[i] for i in range(0, len(xs), 2)]
    return xs[0]


def _pair_words(x):
    hi = pltpu.bitcast(x, jnp.uint32) >> 16
    return hi | (hi << 16)


def _peer_topk_kernel(st_ref, ap_ref, cnt_ref, bp_ref, rank_ref):
    kk = PEER_TOPK
    n_vr = N_KEYS // SUBLANES
    tops = []
    for hs in range(2 * PEER_HEADS):
        x = st_ref[hs]
        lst = _bitonic_sort_desc([x[j * SUBLANES:(j + 1) * SUBLANES, :] for j in range(n_vr)])[:kk]
        shift = SUBLANES // 2
        while shift >= 1:
            other = [pltpu.roll(a, shift, 0) for a in lst]
            lst = _bitonic_merge_desc([jnp.maximum(lst[i], other[kk - 1 - i]) for i in range(kk)])
            shift //= 2
        tops.append(lst)

    sub = lax.broadcasted_iota(jnp.int32, tops[0][0].shape, 0)

    def pack(which, a):
        out = tops[which][a]
        for h in range(1, PEER_HEADS):
            out = jnp.where(sub == h, tops[2 * h + which][a], out)
        return out

    v1 = [pack(0, a) for a in range(kk)]
    v2 = [pack(1, a) for a in range(kk)]
    pairs = [(a, b) for a in range(kk) for b in range(kk) if (a + 1) * (b + 1) <= kk]
    cands = [v1[a] + v2[b] for a, b in pairs]
    cur = list(cands)
    for r in range(kk):
        tau = _tree(jnp.maximum, cur)
        if r + 1 < kk:
            cur = [jnp.where(c == tau, -jnp.inf, c) for c in cur]
    top = v1[0] + v2[0]
    zsum = _tree(jnp.add, [jnp.where(c >= tau, jnp.exp(c - top), 0.0) for c in cands])
    rz = 1.0 / zsum
    cnt_of_rank = [_tree(jnp.add, [jnp.where(c >= tau, 1.0, 0.0) for c, (a2, _) in zip(cands, pairs) if a2 == a])
                   for a in range(kk)]
    for h in range(PEER_HEADS):
        row = lambda a, h=h: a[h:h + 1, :]
        s1 = st_ref[2 * h]
        s2 = st_ref[2 * h + 1]
        ap_ref[h] = _pair_words((jnp.exp(s1 - row(v1[0])) * row(rz)).astype(BF16).astype(F32))
        bp_ref[h] = pltpu.bitcast(jnp.exp(s2 - row(v2[0])).astype(BF16), jnp.uint32)
        cnt = jnp.zeros(s1.shape, F32)
        rank = jnp.full(s2.shape, float(kk), F32)
        for r in range(kk - 1, -1, -1):
            cnt = jnp.where(s1 == row(v1[r]), row(cnt_of_rank[r]), cnt)
            rank = jnp.where(s2 == row(v2[r]), float(r), rank)
        cnt_ref[h] = _pair_words(cnt)
        rank_ref[h] = pltpu.bitcast(rank.astype(BF16), jnp.uint32)


def _peer_topk(st):
    n_hs, n_keys, t = st.shape
    blk = pl.BlockSpec((PEER_HEADS, n_keys, LANES), lambda i: (0, 0, i))
    words = n_keys * 2 // 4
    wblk = pl.BlockSpec((PEER_HEADS, words, LANES), lambda i: (0, 0, i))
    row_out = jax.ShapeDtypeStruct((PEER_HEADS, n_keys, t), jnp.uint32)
    word_out = jax.ShapeDtypeStruct((PEER_HEADS, words, t), jnp.uint32)
    return pl.pallas_call(
        _peer_topk_kernel,
        grid=(t // LANES,),
        in_specs=[pl.BlockSpec((n_hs, n_keys, LANES), lambda i: (0, 0, i))],
        out_specs=[blk, blk, wblk, wblk],
        out_shape=[row_out, row_out, word_out, word_out],
        compiler_params=_params(1),
    )(st)


def _packed_row(words):
    return pltpu.bitcast(jnp.broadcast_to(words, (SUBLANES, LANES)), BF16)


def _gelu_tanh(x):
    c = math.sqrt(2.0 / math.pi)
    return (0.5 * x) * (1.0 + jnp.tanh(x * (c + (c * 0.044715) * (x * x))))


def _peer_dense_kernel(hnt_ref, ap_ref, cnt_ref, bp_ref, rank_ref, u_ref, vt_ref, h_ref, y_ref, acc_scr, hid_scr,
                       w_scr, *, tb, ec):
    e = pl.program_id(1)
    rows_per_step = ec // N_KEYS
    n_pieces = N_KEYS // BF16_ROWS

    @pl.when(e == 0)
    def _():
        acc_scr[...] = jnp.zeros(acc_scr.shape, F32)

    hid_scr[...] = _dot(u_ref[...], hnt_ref[...])
    for tl in range(tb // LANES):
        ts = slice(tl * LANES, (tl + 1) * LANES)
        for i in range(rows_per_step):
            gates = [jnp.zeros((BF16_ROWS, LANES), BF16)] * n_pieces
            for h in range(PEER_HEADS):
                a_row = _packed_row(ap_ref[h, i:i + 1, ts])
                c_row = _packed_row(cnt_ref[h, i:i + 1, ts])
                for j in range(n_pieces):
                    ws = slice(j * SUBLANES, (j + 1) * SUBLANES)
                    prod = a_row * pltpu.bitcast(bp_ref[h, ws, ts], BF16)
                    keep = pltpu.bitcast(rank_ref[h, ws, ts], BF16) < c_row
                    gates[j] = gates[j] + jnp.where(keep, prod, jnp.zeros_like(prod))
            for j in range(n_pieces):
                rs = slice(i * N_KEYS + j * BF16_ROWS, i * N_KEYS + (j + 1) * BF16_ROWS)
                w_scr[rs, ts] = gates[j] * _gelu_tanh(hid_scr[rs, ts].astype(BF16))
    acc_scr[...] += _dot(vt_ref[...], w_scr[...])

    @pl.when(e == pl.num_programs(1) - 1)
    def _():
        y_ref[...] = h_ref[...] + acc_scr[...].T


def _peer_dense(hnt, ap, cnt, bp, rank, u, vt, hres):
    d, t = hnt.shape
    n_exp = u.shape[0]
    tb = min(PEER_TB, t)
    ec = PEER_EC
    kern = functools.partial(_peer_dense_kernel, tb=tb, ec=ec)
    tok3 = pl.BlockSpec((PEER_HEADS, bp.shape[1], tb), lambda i, e: (0, 0, i))
    rows3 = pl.BlockSpec((PEER_HEADS, ec // N_KEYS, tb), lambda i, e: (0, e, i))
    return pl.pallas_call(
        kern,
        grid=(t // tb, n_exp // ec),
        in_specs=[pl.BlockSpec((d, tb), lambda i, e: (0, i)), rows3, rows3, tok3, tok3,
                  pl.BlockSpec((ec, d), lambda i, e: (e, 0)), pl.BlockSpec((d, ec), lambda i, e: (0, e)),
                  pl.BlockSpec((tb, d), lambda i, e: (i, 0))],
        out_specs=pl.BlockSpec((tb, d), lambda i, e: (i, 0)),
        out_shape=jax.ShapeDtypeStruct((t, d), F32),
        scratch_shapes=[pltpu.VMEM((d, tb), F32), pltpu.VMEM((ec, tb), F32), pltpu.VMEM((ec, tb), BF16)],
        compiler_params=_params(2),
    )(hnt, ap, cnt, bp, rank, u, vt, hres)


def _group_matrix(width, group):
    idx = np.arange(width) // group
    return jnp.asarray(idx[:, None] == idx[None, :], dtype=BF16)


def _pad_lanes(vec, offset, width=LANES):
    return jnp.zeros((1, width), F32).at[0, offset:offset + vec.shape[0]].set(vec)


def _layer_weights(l, p):
    d = p["w_in"].shape[1]
    da_qk = DA_HEADS * 2 * DA_HD
    da_v = DA_HEADS * DA_VD
    g_qkv = GDN_HEADS * (2 * GDN_DK + GDN_DV)
    g_z = GDN_HEADS * GDN_DV
    cuts = np.cumsum([0, da_qk, da_qk, da_v, g_qkv, g_z, GDN_HEADS, GDN_HEADS]).tolist()
    w_in = p["w_in"][l].astype(BF16)
    wba = jnp.zeros((d, LANES), BF16).at[:, :2 * GDN_HEADS].set(w_in[:, cuts[5]:cuts[7]])
    w_out = p["w_out"][l].astype(BF16)
    return dict(
        g1=p["norm1_g"][l][None, :], wqk=w_in[:, cuts[0]:cuts[2]], wv=w_in[:, cuts[2]:cuts[3]],
        wg=w_in[:, cuts[3]:cuts[4]], wz=w_in[:, cuts[4]:cuts[5]], wba=wba,
        gq=jnp.tile(p["da_q_g"][l], 2 * DA_HEADS)[None, :], gk=jnp.tile(p["da_k_g"][l], 2 * DA_HEADS)[None, :],
        grp=_group_matrix(da_qk, DA_HD),
        lamv=jnp.stack([p["lambda_q1"][l], p["lambda_k1"][l], p["lambda_q2"][l], p["lambda_k2"][l]]),
        subln=p["da_subln_g"][l][None, :],
        cw=jnp.zeros((SUBLANES, g_qkv), F32).at[:CONV_W].set(p["gdn_conv_w"][l]),
        alog=_pad_lanes(p["gdn_a_log"][l], GDN_HEADS), dtb=_pad_lanes(p["gdn_dt_bias"][l], GDN_HEADS),
        gn=p["gdn_norm_g"][l][None, :],
        tri=jnp.asarray(np.tril(np.ones((GDN_CHUNK, GDN_CHUNK), np.float32))),
        wo1=w_out[:da_v], wo2=w_out[da_v:], g2=p["norm2_g"][l][None, :],
        wq=p["peer_w_q"][l].astype(BF16),
        sk=p["peer_sub_keys"][l].reshape(2 * PEER_HEADS, N_KEYS, -1).astype(BF16),
        u=p["peer_u"][l].astype(BF16), vt=p["peer_v"][l].T.astype(BF16),
    )


def _mixers_to_output(x2, oda, og, w):
    hres, hnt, st = _outproj(x2, oda, og, w["wo1"], w["wo2"], w["g2"], w["wq"], w["sk"])
    ap, cnt, bp, rank = _peer_topk(st)
    return _peer_dense(hnt, ap, cnt, bp, rank, w["u"], w["vt"], hres)


def kernel(x_prompt, x_sample, cache_k, cache_v, state_ssm, state_conv, page_table, rel_bias, norm1_g, w_in,
           da_q_g, da_k_g, lambda_q1, lambda_k1, lambda_q2, lambda_k2, da_subln_g, gdn_conv_w, gdn_a_log,
           gdn_dt_bias, gdn_norm_g, w_out, norm2_g, peer_w_q, peer_sub_keys, peer_u, peer_v):
    p = dict(norm1_g=norm1_g, w_in=w_in, da_q_g=da_q_g, da_k_g=da_k_g, lambda_q1=lambda_q1, lambda_k1=lambda_k1,
             lambda_q2=lambda_q2, lambda_k2=lambda_k2, da_subln_g=da_subln_g, gdn_conv_w=gdn_conv_w,
             gdn_a_log=gdn_a_log, gdn_dt_bias=gdn_dt_bias, gdn_norm_g=gdn_norm_g, w_out=w_out, norm2_g=norm2_g,
             peer_w_q=peer_w_q, peer_sub_keys=peer_sub_keys, peer_u=peer_u, peer_v=peer_v)
    depth = w_in.shape[0]
    bp_, seq, d = x_prompt.shape
    bd, n_new, _ = x_sample.shape
    n_phys = cache_k.shape[1]
    width = DA_HEADS * DA_VD
    g_qkv = GDN_HEADS * (2 * GDN_DK + GDN_DV)
    cache_k2 = jnp.transpose(cache_k, (0, 1, 3, 4, 5, 2)).reshape(depth * n_phys, DA_HEADS, 2, DA_HD, PAGE_SIZE)
    cache_v2 = cache_v.reshape(depth * n_phys, PAGE_SIZE * DA_HEADS, DA_VD)
    rbt = jnp.repeat(rel_bias.T, 2 * SUBLANES, axis=0)
    pad_s = -(-n_new // GDN_CHUNK) * GDN_CHUNK
    zero_conv = jnp.zeros((bp_, SUBLANES, g_qkv), F32)
    zero_ssm = jnp.zeros((bp_,) + state_ssm.shape[2:], F32)

    yp = x_prompt.reshape(bp_ * seq, d)
    ys = x_sample.reshape(bd * n_new, d)
    outs = [[] for _ in range(8)]
    for l in range(depth):
        lambda_init = 0.8 - 0.6 * math.exp(-0.3 * l)
        w = _layer_weights(l, p)
        proj_w = (w["g1"], w["wqk"], w["wv"], w["wg"], w["wz"], w["wba"], w["gq"], w["gk"], w["grp"])
        gdn_w = (w["cw"], w["alog"], w["dtb"], w["gn"], w["tri"])

        q, k, v, qkv, z, ba = _inproj(yp, *proj_w)
        oda = _attn_prompt(q, k, v, rel_bias, w["lamv"], w["subln"], bp_, seq, lambda_init)
        og, ssm_p = _gdn(qkv, z, ba, zero_conv, zero_ssm, *gdn_w, bp_, seq, seq)
        yp = _mixers_to_output(yp, oda, og, w)
        outs[0].append(k.reshape(bp_, seq, DA_HEADS, 2, DA_HD))
        outs[1].append(v.reshape(bp_, seq, DA_HEADS, DA_VD))
        outs[2].append(ssm_p)
        outs[3].append(qkv.reshape(bp_, seq, g_qkv)[:, seq - (CONV_W - 1):])

        q, k, v, qkv, z, ba = _inproj(ys, *proj_w)
        to3 = lambda a: a.reshape(bd, n_new, a.shape[-1])
        q4 = jnp.pad(q.reshape(bd, n_new, 2 * DA_HEADS, DA_HD), ((0, 0), (0, SUBLANES - n_new), (0, 0), (0, 0)))
        qblk = jnp.einsum("btgd,gf->bgtfd", q4, jnp.eye(2 * DA_HEADS, dtype=F32)).reshape(bd, -1, width)
        oda = _attn_sample(qblk, to3(k), to3(v), cache_k2, cache_v2, page_table, l * n_phys, rbt,
                           w["lamv"], w["subln"], lambda_init)[:, :n_new].reshape(bd * n_new, width)
        pad_tok = lambda a: jnp.pad(to3(a), ((0, 0), (0, pad_s - n_new), (0, 0))).reshape(bd * pad_s, a.shape[-1])
        conv0 = jnp.pad(state_conv[l], ((0, 0), (SUBLANES - (CONV_W - 1), 0), (0, 0)))
        og, ssm_s = _gdn(pad_tok(qkv), pad_tok(z), pad_tok(ba), conv0, state_ssm[l], *gdn_w, bd, pad_s, n_new)
        og = og.reshape(bd, pad_s, -1)[:, :n_new].reshape(bd * n_new, -1)
        ys = _mixers_to_output(ys, oda, og, w)
        conv_all = jnp.concatenate([state_conv[l], to3(qkv)], axis=1)
        outs[4].append(k.reshape(bd, n_new, DA_HEADS, 2, DA_HD))
        outs[5].append(v.reshape(bd, n_new, DA_HEADS, DA_VD))
        outs[6].append(ssm_s)
        outs[7].append(conv_all[:, -(CONV_W - 1):])

    return (yp.reshape(bp_, seq, d), ys.reshape(bd, n_new, d)) + tuple(jnp.stack(o) for o in outs)
```
